```python
import math
import jax, jax.numpy as jnp
from jax import lax
import numpy as np

D_MODEL = 1024
BATCH = 2
SEQ = 8192
DEPTH = 2
DEC_BATCH = 128
DEC_SEQ = 4
PAST_LEN = 2048
PAGE_SIZE = 128

HEAD_DIM = 64
MIX_WIDTH = D_MODEL
CONV_WIDTH = MIX_WIDTH // 4
CONV_GROUPS = 4
CONV_K = 3
HGRN_WIDTH = MIX_WIDTH // 4
HGRN_HEADS = 4
HGRN_DK = HGRN_WIDTH // HGRN_HEADS
HGRN_DV = HGRN_WIDTH // HGRN_HEADS
HGRN_CHUNK = 16
FOX_WIDTH = MIX_WIDTH - CONV_WIDTH - HGRN_WIDTH
FOX_HEADS = FOX_WIDTH // HEAD_DIM
Q_BLOCK = 128
D_FF = -(-8 * D_MODEL // (3 * 256)) * 256
SPLIT_SIZES = (CONV_WIDTH, CONV_WIDTH, CONV_WIDTH,
               HGRN_WIDTH, HGRN_WIDTH, HGRN_WIDTH, HGRN_WIDTH,
               FOX_WIDTH, FOX_WIDTH, FOX_WIDTH, FOX_HEADS)
IN_COLS = sum(SPLIT_SIZES)
ALPHA = (2 * DEPTH) ** 0.25
BETA = (8 * DEPTH) ** -0.25
LN_EPS = 1e-5
RMS_EPS = 1e-6
NEG_BIG = -1e30

kernel_name = 'hybrid_conv_hgrn2_fox_decoder_step'


def layer_norm(x, g, b):
    xf = x.astype(jnp.float32)
    mu = jnp.mean(xf, axis=-1, keepdims=True)
    var = jnp.mean(jnp.square(xf - mu), axis=-1, keepdims=True)
    y = (xf - mu) * lax.rsqrt(var + LN_EPS)
    return (y * g.astype(jnp.float32) + b.astype(jnp.float32)).astype(x.dtype)


def hgrn_lower_bounds(lb_logits):
    p = jax.nn.softmax(lb_logits.astype(jnp.float32), axis=0)
    return jnp.cumsum(p, axis=0) - p


def short_conv(u, prev, w):
    L = u.shape[1]
    ext = jnp.concatenate([prev.astype(u.dtype), u], axis=1)
    y = w[0] * ext[:, 0:L]
    for j in range(1, CONV_K):
        y = y + w[j] * ext[:, j:j + L]
    return y, ext[:, L:]


def hgrn_recurrence(q, k, v, logf, s0):
    bsz, L, H, dk = q.shape
    dv = v.shape[-1]
    C = math.gcd(L, HGRN_CHUNK)
    n = L // C

    def chunks(a):
        return a.astype(jnp.float32).reshape(bsz, n, C, H, a.shape[-1]).transpose(1, 0, 3, 2, 4)

    causal = jnp.tril(jnp.ones((C, C), dtype=bool))[None, None, :, :, None]

    def step(S, inp):
        qc, kc, vc, gc = inp
        G = jnp.cumsum(gc, axis=2)
        diff = G[:, :, :, None, :] - G[:, :, None, :, :]
        decay = jnp.where(causal, jnp.exp(jnp.minimum(diff, 0.0)), 0.0)
        A = jnp.einsum('bhtd,bhsd,bhtsd->bhts', qc, kc, decay)
        o = jnp.einsum('bhts,bhsv->bhtv', A, vc) + jnp.einsum('bhtd,bhdv->bhtv', qc * jnp.exp(G), S)
        G_last = G[:, :, -1:, :]
        S = jnp.exp(G_last[:, :, 0, :])[..., None] * S + jnp.einsum('bhsd,bhsv->bhdv', kc * jnp.exp(G_last - G), vc)
        return S, o

    S_fin, o = lax.scan(step, s0.astype(jnp.float32), (chunks(q), chunks(k), chunks(v), chunks(logf)))
    o = o.transpose(1, 0, 3, 2, 4).reshape(bsz, L, H, dv)
    return o, S_fin


def fox_attend(q, k, v, cq, ck, qpos, kpos):
    s = jnp.einsum('bqhd,bkhd->bhqk', q, k, preferred_element_type=jnp.float32) * (HEAD_DIM ** -0.5)
    s = s + jnp.swapaxes(cq, 1, 2)[:, :, :, None] - jnp.swapaxes(ck, 1, 2)[:, :, None, :]
    s = jnp.where((kpos[None, :] <= qpos[:, None])[None, None], s, NEG_BIG)
    p = jax.nn.softmax(s, axis=-1)
    return jnp.einsum('bhqk,bkhd->bqhd', p.astype(v.dtype), v, preferred_element_type=jnp.float32)


def fox_prompt(q, k, v, logf):
    bsz, L = q.shape[:2]
    c = jnp.cumsum(logf, axis=1)
    kpos = jnp.arange(L)

    def block(i):
        start = i * Q_BLOCK
        qb = lax.dynamic_slice_in_dim(q, start, Q_BLOCK, axis=1)
        cb = lax.dynamic_slice_in_dim(c, start, Q_BLOCK, axis=1)
        return fox_attend(qb, k, v, cb, c, start + jnp.arange(Q_BLOCK), kpos)

    out = lax.map(block, jnp.arange(L // Q_BLOCK))
    return out.transpose(1, 0, 2, 3, 4).reshape(bsz, L, FOX_HEADS, HEAD_DIM)


def fox_sample(q, k, v, logf, k_past, v_past, logf_past):
    past = k_past.shape[1]
    L = q.shape[1]
    k_all = jnp.concatenate([k_past.astype(k.dtype), k], axis=1)
    v_all = jnp.concatenate([v_past.astype(v.dtype), v], axis=1)
    c_all = jnp.cumsum(jnp.concatenate([logf_past.astype(jnp.float32), logf], axis=1), axis=1)
    return fox_attend(q, k_all, v_all, c_all[:, past:], c_all, past + jnp.arange(L), jnp.arange(past + L))


def gather_pages(pool, page_table):
    g = pool[page_table]
    return g.reshape((page_table.shape[0], -1) + pool.shape[2:])


def run_group(x, c, past, ln_in_g, ln_in_b, w_mod, b_mod, w_in, conv_w, lower_bounds, hgrn_norm_g,
              fox_f_bias, w_out, w_ffn_in, w_ffn_out, ln_g, ln_b):
    bsz, L, _ = x.shape
    offsets = [int(o) for o in np.cumsum(SPLIT_SIZES)[:-1]]
    x = layer_norm(x, ln_in_g, ln_in_b)
    ks, vs, lfs, Ss, convs = [], [], [], [], []
    for l in range(DEPTH):
        mod = (jax.nn.silu(c) @ w_mod[l] + b_mod[l])[:, None, :]
        sh1, sc1, g1, sh2, sc2, g2 = jnp.split(mod, 6, axis=-1)
        h = x * (1 + sc1) + sh1
        proj = h @ w_in[l]
        (cb, cc, ch, hq, hf, hi, hg, fq, fk, fv, ff) = jnp.split(proj, offsets, axis=-1)

        if past is None:
            conv_prev = jnp.zeros((bsz, CONV_K - 1, CONV_WIDTH), x.dtype)
            s0 = jnp.zeros((bsz, HGRN_HEADS, HGRN_DK, HGRN_DV), jnp.float32)
        else:
            conv_prev = past[4][l]
            s0 = past[3][l]

        y_conv, conv_new = short_conv(cc * ch, conv_prev, conv_w[l])
        y_conv = cb * y_conv

        lb = lower_bounds[l]
        has_lb = lb > 0
        log_lb = jnp.where(has_lb, jnp.log(jnp.where(has_lb, lb, 1.0)), NEG_BIG)
        logf_h = jnp.logaddexp(log_lb, jnp.log1p(-lb) + jax.nn.log_sigmoid(hf.astype(jnp.float32)))
        k_h = -jnp.expm1(logf_h)
        q_h = jax.nn.silu(hq)
        o_h, S_fin = hgrn_recurrence(q_h.reshape(bsz, L, HGRN_HEADS, HGRN_DK),
                                     k_h.reshape(bsz, L, HGRN_HEADS, HGRN_DK),
                                     hi.reshape(bsz, L, HGRN_HEADS, HGRN_DV),
                                     logf_h.reshape(bsz, L, HGRN_HEADS, HGRN_DK), s0)
        o_h = o_h * lax.rsqrt(jnp.mean(jnp.square(o_h), axis=-1, keepdims=True) + RMS_EPS)
        y_hgrn = (o_h.reshape(bsz, L, HGRN_WIDTH) * hgrn_norm_g[l].astype(jnp.float32)
                  * jax.nn.silu(hg.astype(jnp.float32))).astype(x.dtype)

        fq = fq.reshape(bsz, L, FOX_HEADS, HEAD_DIM)
        fk = fk.reshape(bsz, L, FOX_HEADS, HEAD_DIM)
        fv = fv.reshape(bsz, L, FOX_HEADS, HEAD_DIM)
        logf_f = jax.nn.log_sigmoid(ff.astype(jnp.float32) + fox_f_bias[l].astype(jnp.float32))
        if past is None:
            y_fox = fox_prompt(fq, fk, fv, logf_f)
        else:
            page_table = past[5]
            y_fox = fox_sample(fq, fk, fv, logf_f,
                               gather_pages(past[0][l], page_table),
                               gather_pages(past[1][l], page_table),
                               gather_pages(past[2][l], page_table))
        y_fox = y_fox.reshape(bsz, L, FOX_WIDTH).astype(x.dtype)

        mix = jnp.concatenate([y_conv, y_hgrn, y_fox], axis=-1) @ w_out[l]
        x = layer_norm(ALPHA * x + (1 + g1) * mix, ln_g[l, 0], ln_b[l, 0])

        h2 = x * (1 + sc2) + sh2
        gate, up = jnp.split(h2 @ w_ffn_in[l], 2, axis=-1)
        ffn = (jax.nn.silu(gate) * up) @ w_ffn_out[l]
        x = layer_norm(ALPHA * x + (1 + g2) * ffn, ln_g[l, 1], ln_b[l, 1])

        ks.append(fk)
        vs.append(fv)
        lfs.append(logf_f)
        Ss.append(S_fin)
        convs.append(conv_new)
    return x, jnp.stack(ks), jnp.stack(vs), jnp.stack(lfs), jnp.stack(Ss), jnp.stack(convs)


def setup_inputs(seed: int = 0) -> dict:
    key = jax.random.key(seed)
    k = jax.random.split(key, 32)
    f32 = jnp.float32
    n_pages = PAST_LEN // PAGE_SIZE
    n_used = DEC_BATCH * n_pages
    n_phys = n_used + max(1, n_used // 4)
    nrm = lambda i, shape: jax.random.normal(k[i], shape, f32)
    page_table = jax.random.permutation(k[9], n_phys)[:n_used].reshape(DEC_BATCH, n_pages).astype(jnp.int32)
    return {
        'x_prompt': nrm(0, (BATCH, SEQ, D_MODEL)),
        'x_sample': nrm(1, (DEC_BATCH, DEC_SEQ, D_MODEL)),
        'c_prompt': nrm(2, (BATCH, D_MODEL)),
        'c_sample': nrm(3, (DEC_BATCH, D_MODEL)),
        'cache_k': nrm(4, (DEPTH, n_phys, PAGE_SIZE, FOX_HEADS, HEAD_DIM)),
        'cache_v': nrm(5, (DEPTH, n_phys, PAGE_SIZE, FOX_HEADS, HEAD_DIM)),
        'cache_logf': jax.nn.log_sigmoid(2.0 + 0.5 * nrm(6, (DEPTH, n_phys, PAGE_SIZE, FOX_HEADS))),
        'state_hgrn': 0.5 * nrm(7, (DEPTH, DEC_BATCH, HGRN_HEADS, HGRN_DK, HGRN_DV)),
        'state_conv': nrm(8, (DEPTH, DEC_BATCH, CONV_K - 1, CONV_WIDTH)),
        'page_table': page_table,
        'ln_in_g': 1.0 + 0.01 * nrm(10, (D_MODEL,)),
        'ln_in_b': 0.01 * nrm(11, (D_MODEL,)),
        'w_mod': 0.25 * D_MODEL ** -0.5 * nrm(12, (DEPTH, D_MODEL, 6 * D_MODEL)),
        'b_mod': 0.01 * nrm(13, (DEPTH, 6 * D_MODEL)),
        'w_in': D_MODEL ** -0.5 * nrm(14, (DEPTH, D_MODEL, IN_COLS)),
        'conv_w': CONV_K ** -0.5 * nrm(15, (DEPTH, CONV_K, CONV_WIDTH)),
        'hgrn_lb_logits': 0.5 * nrm(16, (DEPTH, HGRN_WIDTH)),
        'hgrn_norm_g': 1.0 + 0.01 * nrm(17, (DEPTH, HGRN_WIDTH)),
        'fox_f_bias': 2.0 + 0.1 * nrm(18, (DEPTH, FOX_HEADS)),
        'w_out': BETA * MIX_WIDTH ** -0.5 * nrm(19, (DEPTH, MIX_WIDTH, D_MODEL)),
        'w_ffn_in': D_MODEL ** -0.5 * nrm(20, (DEPTH, D_MODEL, 2 * D_FF)),
        'w_ffn_out': BETA * D_FF ** -0.5 * nrm(21, (DEPTH, D_FF, D_MODEL)),
        'ln_g': 1.0 + 0.01 * nrm(22, (DEPTH, 2, D_MODEL)),
        'ln_b': 0.01 * nrm(23, (DEPTH, 2, D_MODEL)),
    }


def reference(x_prompt, x_sample, c_prompt, c_sample, cache_k, cache_v, cache_logf, state_hgrn, state_conv,
              page_table, ln_in_g, ln_in_b, w_mod, b_mod, w_in, conv_w, hgrn_lb_logits, hgrn_norm_g,
              fox_f_bias, w_out, w_ffn_in, w_ffn_out, ln_g, ln_b):
    lower_bounds = hgrn_lower_bounds(hgrn_lb_logits)
    shared = (ln_in_g, ln_in_b, w_mod, b_mod, w_in, conv_w, lower_bounds, hgrn_norm_g,
              fox_f_bias, w_out, w_ffn_in, w_ffn_out, ln_g, ln_b)
    y_prompt, k_p, v_p, lf_p, hgrn_p, conv_p = run_group(x_prompt, c_prompt, None, *shared)
    past = (cache_k, cache_v, cache_logf, state_hgrn, state_conv, page_table)
    y_sample, k_s, v_s, lf_s, hgrn_s, conv_s = run_group(x_sample, c_sample, past, *shared)
    return (y_prompt, y_sample, k_p, v_p, lf_p, hgrn_p, conv_p, k_s, v_s, lf_s, hgrn_s, conv_s)
```

```python
import functools

import jax
import jax.numpy as jnp
from jax import lax
from jax.experimental import pallas as pl
from jax.experimental.pallas import tpu as pltpu

F32 = jnp.float32
BF16 = jnp.bfloat16

HEAD_DIM = 64
HGRN_HEADS = 4
CONV_K = 3
LN_EPS = 1e-5
RMS_EPS = 1e-6
NEG_BIG = -1e30

LANES = 128
SUBLANES = 8
VMEM_LIMIT = 56 * 1024 * 1024
TOKEN_TILE = 512
HGRN_CHUNK = 128
MOD_TILE_N = 1536

NT_DIMS = (((1,), (1,)), ((), ()))
TN_DIMS = (((0,), (0,)), ((), ()))


def _params(semantics):
    return pltpu.CompilerParams(dimension_semantics=semantics, vmem_limit_bytes=VMEM_LIMIT)


def _layer_norm(x, g, b):
    mu = jnp.mean(x, axis=-1, keepdims=True)
    xc = x - mu
    var = jnp.mean(xc * xc, axis=-1, keepdims=True)
    return xc * lax.rsqrt(var + LN_EPS) * g + b


def _log_sigmoid(x):
    return -(jnp.maximum(-x, 0.0) + jnp.log1p(jnp.exp(-jnp.abs(x))))


def _split3(x):
    hi = x.astype(BF16)
    r = x - hi.astype(F32)
    mid = r.astype(BF16)
    lo = (r - mid.astype(F32)).astype(BF16)
    return hi, mid, lo


def _dot_exact_lhs(a, parts):
    return sum(jnp.dot(a, p, preferred_element_type=F32) for p in parts)


def _iota(shape, axis):
    return lax.broadcasted_iota(jnp.int32, shape, axis)


def _log2(n):
    assert n > 0 and n & (n - 1) == 0, n
    return n.bit_length() - 1


def _div(x, n):
    return lax.shift_right_logical(x, _log2(n))


def _mod(x, n):
    return jnp.bitwise_and(x, n - 1)


def _mod_kernel(c_ref, w_ref, b_ref, o_ref):
    c = c_ref[...]
    a = (c * jax.nn.sigmoid(c)).astype(BF16)
    o_ref[...] = jnp.dot(a, w_ref[...].astype(BF16), preferred_element_type=F32) + b_ref[...]


def _modulation(c_all, w_mod, b_mod):
    depth, d, n = w_mod.shape
    rows = c_all.shape[0]
    return pl.pallas_call(
        _mod_kernel,
        grid=(depth, n // MOD_TILE_N),
        in_specs=[
            pl.BlockSpec((rows, d), lambda l, j: (0, 0)),
            pl.BlockSpec((None, d, MOD_TILE_N), lambda l, j: (l, 0, j)),
            pl.BlockSpec((None, 1, MOD_TILE_N), lambda l, j: (l, 0, j)),
        ],
        out_specs=pl.BlockSpec((None, rows, MOD_TILE_N), lambda l, j: (l, 0, j)),
        out_shape=jax.ShapeDtypeStruct((depth, rows, n), F32),
        compiler_params=_params(("arbitrary", "arbitrary")),
        name="modulation",
    )(c_all, w_mod, b_mod.reshape(depth, 1, n))


def _inproj_kernel(*refs, layer, sample, first_layer, tiles_per_seq, d_model, conv_w, hgrn_w, fox_w):
    it = iter(refs)
    x_ref, mod_ref = next(it), next(it)
    if first_layer:
        lng_ref, lnb_ref = next(it), next(it)
    w_ref, cw_ref, lbl_ref, fb_ref = next(it), next(it), next(it), next(it)
    if sample:
        prev_ref = next(it)
    (yconv_ref, u_ref, qh_ref, kh_ref, lfh_ref, vh_ref, hg_ref,
     fq_ref, fk_ref, fv_ref, fkb_ref, fvb_ref, lff_ref) = (next(it) for _ in range(13))
    if not sample:
        c_ref, ct_ref, carry_ref, ccarry_ref = next(it), next(it), next(it), next(it)

    tm = x_ref.shape[0]
    x = x_ref[...]
    if first_layer:
        x = _layer_norm(x, lng_ref[...], lnb_ref[...])
    m = mod_ref[...]
    if sample:
        m = jnp.concatenate([m] * (tm // m.shape[0]), axis=0)
    sh1, sc1 = m[:, :d_model], m[:, d_model:]
    h = (x * (1.0 + sc1) + sh1).astype(BF16)

    c0 = 0
    pc = jnp.dot(h, w_ref[:, c0:c0 + 3 * conv_w], preferred_element_type=F32)
    cb, cc, ch = pc[:, :conv_w], pc[:, conv_w:2 * conv_w], pc[:, 2 * conv_w:]
    u = cc * ch
    if sample:
        nb = tm // 4
        prev0, prev1 = prev_ref[:nb, :], prev_ref[nb:, :]
        u1 = jnp.concatenate([prev1, u[:tm - nb]], axis=0)
        u2 = jnp.concatenate([prev0, prev1, u[:tm - 2 * nb]], axis=0)
    else:
        i = pl.program_id(0)

        @pl.when(i % tiles_per_seq == 0)
        def _():
            carry_ref[...] = jnp.zeros_like(carry_ref)
            ccarry_ref[...] = jnp.zeros_like(ccarry_ref)

        prev = carry_ref[...]
        row = _iota((tm, 1), 0)
        last1, last2 = prev[SUBLANES - 1:SUBLANES], prev[SUBLANES - 2:SUBLANES - 1]
        u1 = jnp.where(row == 0, last1, pltpu.roll(u, 1, 0))
        u2 = jnp.where(row == 0, last2, jnp.where(row == 1, last1, pltpu.roll(u, 2, 0)))
        carry_ref[...] = u[tm - SUBLANES:]
    cw = cw_ref[...]
    yconv_ref[...] = (cb * (cw[0:1] * u2 + cw[1:2] * u1 + cw[2:3] * u)).astype(BF16)
    u_ref[...] = u

    c0 += 3 * conv_w
    ph = jnp.dot(h, w_ref[:, c0:c0 + 4 * hgrn_w], preferred_element_type=F32)
    hq, hf = ph[:, :hgrn_w], ph[:, hgrn_w:2 * hgrn_w]
    hi, hg = ph[:, 2 * hgrn_w:3 * hgrn_w], ph[:, 3 * hgrn_w:]
    logits = lbl_ref[...]
    e = jnp.exp(logits - jnp.max(logits, axis=0, keepdims=True))
    p = e / jnp.sum(e, axis=0, keepdims=True)
    csum = p[0:1]
    for j in range(1, layer + 1):
        csum = csum + p[j:j + 1]
    lb = csum - p[layer:layer + 1]
    has_lb = lb > 0
    log_lb = jnp.where(has_lb, jnp.log(jnp.where(has_lb, lb, 1.0)), NEG_BIG)
    b2 = jnp.log1p(-lb) + _log_sigmoid(hf)
    lfh_ref[...] = jnp.maximum(log_lb, b2) + jnp.log1p(jnp.exp(-jnp.abs(log_lb - b2)))
    kh_ref[...] = (1.0 - lb) * jax.nn.sigmoid(-hf)
    qh_ref[...] = hq * jax.nn.sigmoid(hq)
    vh_ref[...] = hi
    hg_ref[...] = hg

    c0 += 4 * hgrn_w
    pf = jnp.dot(h, w_ref[:, c0:c0 + 3 * fox_w], preferred_element_type=F32)
    fq, fk, fv = pf[:, :fox_w], pf[:, fox_w:2 * fox_w], pf[:, 2 * fox_w:]
    fq_ref[...] = (fq * (HEAD_DIM ** -0.5)).astype(BF16)
    fk_ref[...] = fk
    fv_ref[...] = fv
    fkb_ref[...] = fk.astype(BF16)
    fvb_ref[...] = fv.astype(BF16)
    c0 += 3 * fox_w
    pff = jnp.dot(h, w_ref[:, c0:c0 + LANES], preferred_element_type=F32)
    lff = _log_sigmoid(pff + fb_ref[...])
    nh = lff_ref.shape[1]
    lff_ref[...] = lff[:, :nh]
    if not sample:
        tri = (_iota((tm, tm), 0) >= _iota((tm, tm), 1)).astype(BF16)
        c = _dot_exact_lhs(tri, _split3(lff)) + ccarry_ref[0:1, :]
        ccarry_ref[...] = jnp.broadcast_to(c[tm - 1:tm], ccarry_ref.shape)
        c_ref[...] = c[:, :nh]
        ct_ref[...] = c.T[:nh]


def _inproj(x, mod, ln_in, w_in_p, conv_w, lb_logits, fbias, prev, *, layer, n_seq, sample):
    t, d = x.shape
    tm = TOKEN_TILE
    n_tiles = t // tm
    tiles_per_seq = (t // n_seq) // tm if not sample else 1
    convw = conv_w.shape[1]
    hgw = lb_logits.shape[1]
    ncols = w_in_p.shape[1]
    foxw = (ncols - LANES - 3 * convw - 4 * hgw) // 3
    nh = foxw // HEAD_DIM
    first_layer = ln_in is not None
    depth = mod.shape[0]

    inputs, in_specs = [x], [pl.BlockSpec((tm, d), lambda i: (i, 0))]
    if sample:
        nb = t // 4
        inputs.append(mod)
        in_specs.append(pl.BlockSpec((None, nb, 2 * d), lambda i: (layer, 0, 0)))
    else:
        rows = mod.shape[1]
        inputs.append(mod.reshape(depth, rows, 1, mod.shape[2]))
        in_specs.append(pl.BlockSpec((None, None, 1, 2 * d),
                                     lambda i: (layer, rows - SUBLANES + i // tiles_per_seq, 0, 0)))
    if first_layer:
        inputs += [ln_in[0].reshape(1, d), ln_in[1].reshape(1, d)]
        in_specs += [pl.BlockSpec((1, d), lambda i: (0, 0))] * 2
    inputs += [w_in_p, conv_w, lb_logits, fbias]
    in_specs += [
        pl.BlockSpec((d, ncols), lambda i: (0, 0), pipeline_mode=pl.Buffered(1)),
        pl.BlockSpec(conv_w.shape, lambda i: (0, 0)),
        pl.BlockSpec(lb_logits.shape, lambda i: (0, 0)),
        pl.BlockSpec((1, LANES), lambda i: (0, 0)),
    ]
    if sample:
        inputs.append(prev)
        in_specs.append(pl.BlockSpec(prev.shape, lambda i: (0, 0)))

    tok = lambda w, dt: (jax.ShapeDtypeStruct((t, w), dt), pl.BlockSpec((tm, w), lambda i: (i, 0)))
    outs = [tok(convw, BF16), tok(convw, F32)] + [tok(hgw, F32)] * 5 + [
        tok(foxw, BF16), tok(foxw, F32), tok(foxw, F32), tok(foxw, BF16), tok(foxw, BF16), tok(nh, F32)]
    scratch = []
    if not sample:
        seq = t // n_seq
        outs.append(tok(nh, F32))
        outs.append((jax.ShapeDtypeStruct((n_seq, nh, seq), F32),
                     pl.BlockSpec((None, nh, tm), lambda i: (i // tiles_per_seq, 0, i % tiles_per_seq))))
        scratch = [pltpu.VMEM((SUBLANES, convw), F32), pltpu.VMEM((SUBLANES, LANES), F32)]
    kern = functools.partial(_inproj_kernel, layer=layer, sample=sample, first_layer=first_layer,
                             tiles_per_seq=tiles_per_seq, d_model=d, conv_w=convw, hgrn_w=hgw, fox_w=foxw)
    return pl.pallas_call(
        kern,
        grid=(n_tiles,),
        in_specs=in_specs,
        out_specs=[o[1] for o in outs],
        out_shape=[o[0] for o in outs],
        scratch_shapes=scratch,
        compiler_params=_params(("arbitrary",)),
        name="inproj_sample" if sample else "inproj_prompt",
    )(*inputs)


def _hgrn_prompt_kernel(q_ref, k_ref, g_ref, v_ref, o_ref, st_out_ref, st_ref, *, chunk, n_heads):
    i = pl.program_id(1)
    tm, w = q_ref.shape
    dh = w // n_heads
    c = chunk

    @pl.when(i == 0)
    def _():
        st_ref[...] = jnp.zeros_like(st_ref)

    r_cc, c_cc = _iota((c, c), 0), _iota((c, c), 1)
    tri = (r_cc >= c_cc).astype(BF16)
    lane = _iota((1, w), 1)
    head_masks = [_div(lane, dh) == hh for hh in range(n_heads)]
    bd = _div(_iota((w, w), 0), dh) == _div(_iota((w, w), 1), dh)
    bd_bf = bd.astype(BF16)
    row = _iota((c, 1), 0)
    col_key = _mod(_iota((c, n_heads * c), 1), c)
    row_q = _iota((c, n_heads * c), 0)
    levels = []
    b = c // 2
    while b >= 1:
        ref_row = _div(r_cc, 2 * b) * (2 * b) + (b - 1)
        levels.append(dict(
            sel=(c_cc == ref_row).astype(BF16),
            upper=_mod(row, 2 * b) >= b,
            lower=_mod(row, 2 * b) < b,
            same=_div(row_q, 2 * b) == _div(col_key, 2 * b),
        ))
        b //= 2

    def chunk_body(ci, carry):
        r0 = pl.multiple_of(ci * c, c)
        q = q_ref[pl.ds(r0, c), :]
        k = k_ref[pl.ds(r0, c), :]
        g = g_ref[pl.ds(r0, c), :]
        v = v_ref[pl.ds(r0, c), :]
        vb = v.astype(BF16)
        gcum = _dot_exact_lhs(tri, _split3(g))
        gparts = _split3(gcum)
        a = jnp.zeros((c, n_heads * c), F32)
        for lv in levels:
            gref = _dot_exact_lhs(lv["sel"], gparts)
            eq = jnp.exp(jnp.where(lv["upper"], gcum - gref, NEG_BIG))
            ek = jnp.exp(jnp.where(lv["lower"], gref - gcum, NEG_BIG))
            qt = (q * eq).astype(BF16)
            kt = (k * ek).astype(BF16)
            kbd = jnp.concatenate([jnp.where(hm, kt, jnp.zeros_like(kt)) for hm in head_masks], axis=0)
            xl = lax.dot_general(qt, kbd, NT_DIMS, preferred_element_type=F32)
            a = jnp.where(lv["same"], xl, a)
        qk_hi = (q * k).astype(BF16)
        qk_lo = (q * k - qk_hi.astype(F32)).astype(BF16)
        diag = (jnp.dot(qk_hi, bd_bf, preferred_element_type=F32)
                + jnp.dot(qk_lo, bd_bf, preferred_element_type=F32))
        o = diag * v
        vbd = jnp.concatenate([jnp.where(hm, vb, jnp.zeros_like(vb)) for hm in head_masks], axis=0)
        o = o + jnp.dot(a.astype(BF16), vbd, preferred_element_type=F32)
        st = st_ref[...]
        qg = (q * jnp.exp(gcum)).astype(BF16)
        o = o + lax.dot_general(qg, st.astype(BF16), NT_DIMS, preferred_element_type=F32)
        o_ref[pl.ds(r0, c), :] = o
        g_last = gcum[c - 1:c]
        kd = (k * jnp.exp(g_last - gcum)).astype(BF16)
        upd = lax.dot_general(vb, kd, TN_DIMS, preferred_element_type=F32)
        st_ref[...] = st * jnp.exp(g_last) + jnp.where(bd, upd, 0.0)
        return carry

    lax.fori_loop(0, tm // c, chunk_body, 0)

    @pl.when(i == pl.num_programs(1) - 1)
    def _():
        st_out_ref[...] = st_ref[...].T


def _hgrn_prompt(qh, kh, lfh, vh, *, n_seq):
    t, w = qh.shape
    tm = TOKEN_TILE
    tps = (t // n_seq) // tm
    spec = pl.BlockSpec((tm, w), lambda b, i: (b * tps + i, 0))
    kern = functools.partial(_hgrn_prompt_kernel, chunk=HGRN_CHUNK, n_heads=HGRN_HEADS)
    return pl.pallas_call(
        kern,
        grid=(n_seq, tps),
        in_specs=[spec] * 4,
        out_specs=[spec, pl.BlockSpec((None, w, w), lambda b, i: (b, 0, 0))],
        out_shape=[jax.ShapeDtypeStruct((t, w), F32), jax.ShapeDtypeStruct((n_seq, w, w), F32)],
        scratch_shapes=[pltpu.VMEM((w, w), F32)],
        compiler_params=_params(("arbitrary", "arbitrary")),
        name="hgrn_prompt",
    )(qh, kh, lfh, vh)


def _hgrn_sample_kernel(q_ref, k_ref, g_ref, v_ref, s0_ref, o_ref, s_out_ref, *, n_steps, n_heads):
    nb = s0_ref.shape[0]
    w = q_ref.shape[1]
    dh = w // n_heads
    hd = dh * dh
    lane = _iota((dh, hd), 1)
    rsel = _iota((dh, hd), 0)
    spread_d = (_div(lane, dh) == rsel).astype(BF16)
    spread_v = (_mod(lane, dh) == rsel).astype(BF16)
    for hh in range(n_heads):
        ls, le = hh * dh, (hh + 1) * dh
        s = s0_ref[:, hh * hd:(hh + 1) * hd]
        for step in range(n_steps):
            rows = slice(step * nb, (step + 1) * nb)
            f = jnp.exp(g_ref[rows, ls:le])
            f_x = _dot_exact_lhs_r(_split3(f), spread_d)
            k_x = jnp.dot(k_ref[rows, ls:le].astype(BF16), spread_d, preferred_element_type=F32)
            v_x = jnp.dot(v_ref[rows, ls:le].astype(BF16), spread_v, preferred_element_type=F32)
            q_x = jnp.dot(q_ref[rows, ls:le].astype(BF16), spread_d, preferred_element_type=F32)
            s = f_x * s + k_x * v_x
            o = lax.dot_general((q_x * s).astype(BF16), spread_v, NT_DIMS, preferred_element_type=F32)
            o_ref[rows, ls:le] = o
        s_out_ref[:, hh * hd:(hh + 1) * hd] = s


def _dot_exact_lhs_r(parts, a):
    return sum(jnp.dot(p, a, preferred_element_type=F32) for p in parts)


def _hgrn_sample(qh, kh, lfh, vh, s0, *, n_steps):
    t, w = qh.shape
    nb = t // n_steps
    full = lambda shape: pl.BlockSpec(shape, lambda i: (0,) * len(shape))
    kern = functools.partial(_hgrn_sample_kernel, n_steps=n_steps, n_heads=HGRN_HEADS)
    return pl.pallas_call(
        kern,
        grid=(1,),
        in_specs=[full((t, w))] * 4 + [full(s0.shape)],
        out_specs=[full((t, w)), full(s0.shape)],
        out_shape=[jax.ShapeDtypeStruct((t, w), F32), jax.ShapeDtypeStruct(s0.shape, F32)],
        compiler_params=_params(("arbitrary",)),
        name="hgrn_sample",
    )(qh, kh, lfh, vh, s0)


def _fox_prompt_kernel(q_ref, k_ref, v_ref, c_ref, ct_ref, o_ref, m_scr, l_scr, acc_scr):
    qi, kj = pl.program_id(1), pl.program_id(2)
    tq, tk = q_ref.shape[0], k_ref.shape[0]
    nh = c_ref.shape[1]
    dh = HEAD_DIM

    @pl.when(kj == 0)
    def _():
        m_scr[...] = jnp.full_like(m_scr, NEG_BIG)
        l_scr[...] = jnp.zeros_like(l_scr)
        acc_scr[...] = jnp.zeros_like(acc_scr)

    def tile(masked):
        if masked:
            keep = _iota((tq, tk), 1) <= _iota((tq, tk), 0)
        for hh in range(nh):
            ls = slice(hh * dh, (hh + 1) * dh)
            s = lax.dot_general(q_ref[:, ls], k_ref[:, ls], NT_DIMS, preferred_element_type=F32)
            s = s + (c_ref[:, hh:hh + 1] - ct_ref[hh:hh + 1, :])
            if masked:
                s = jnp.where(keep, s, NEG_BIG)
            m_prev = m_scr[hh]
            m_next = jnp.maximum(m_prev, jnp.max(s, axis=1, keepdims=True))
            p = jnp.exp(s - jnp.concatenate([m_next] * (tk // LANES), axis=1))
            alpha = jnp.exp(m_prev - m_next)
            l_scr[hh] = alpha * l_scr[hh] + jnp.sum(p, axis=1, keepdims=True)
            pv = jnp.dot(p.astype(BF16), v_ref[:, ls], preferred_element_type=F32)
            acc_scr[hh] = acc_scr[hh] * alpha[:, :dh] + pv
            m_scr[hh] = m_next

    pl.when(kj < qi)(lambda: tile(False))
    pl.when(kj == qi)(lambda: tile(True))

    @pl.when(kj == qi)
    def _():
        for hh in range(nh):
            o_ref[:, hh * dh:(hh + 1) * dh] = (acc_scr[hh] / l_scr[hh][:, :dh]).astype(BF16)


def _fox_prompt(fq, fkb, fvb, c, ct, *, n_seq):
    t, w = fq.shape
    nh = c.shape[1]
    tq = tk = TOKEN_TILE
    nq = (t // n_seq) // tq
    kv_spec = pl.BlockSpec((tk, w), lambda b, i, j: (b * nq + jnp.minimum(i, j), 0))
    return pl.pallas_call(
        _fox_prompt_kernel,
        grid=(n_seq, nq, nq),
        in_specs=[
            pl.BlockSpec((tq, w), lambda b, i, j: (b * nq + i, 0)),
            kv_spec, kv_spec,
            pl.BlockSpec((tq, nh), lambda b, i, j: (b * nq + i, 0)),
            pl.BlockSpec((None, nh, tk), lambda b, i, j: (b, 0, jnp.minimum(i, j))),
        ],
        out_specs=pl.BlockSpec((tq, w), lambda b, i, j: (b * nq + i, 0)),
        out_shape=jax.ShapeDtypeStruct((t, w), BF16),
        scratch_shapes=[pltpu.VMEM((nh, tq, LANES), F32), pltpu.VMEM((nh, tq, LANES), F32),
                        pltpu.VMEM((nh, tq, HEAD_DIM), F32)],
        compiler_params=_params(("arbitrary", "arbitrary", "arbitrary")),
        name="fox_prompt",
    )(fq, fkb, fvb, c, ct)


def _page_sums_kernel(x_ref, o_ref, *, page, n_heads, suffix):
    n = page * n_heads
    src, dst = _iota((n, n), 0), _iota((n, n), 1)
    src_r, src_h = _div(src, n_heads), _mod(src, n_heads)
    dst_h, dst_r = _div(dst, page), _mod(dst, page)
    inside = (src_r >= dst_r) if suffix else (src_r <= dst_r)
    mat = ((src_h == dst_h) & inside).astype(BF16)
    o_ref[...] = _dot_exact_lhs_r(_split3(x_ref[...]), mat)


def _page_sums(x, *, page, n_heads, suffix):
    rows, n = x.shape
    tr = 256 if rows % 256 == 0 else rows
    kern = functools.partial(_page_sums_kernel, page=page, n_heads=n_heads, suffix=suffix)
    return pl.pallas_call(
        kern,
        grid=(rows // tr,),
        in_specs=[pl.BlockSpec((tr, n), lambda i: (i, 0))],
        out_specs=pl.BlockSpec((tr, n), lambda i: (i, 0)),
        out_shape=jax.ShapeDtypeStruct((rows, n), F32),
        compiler_params=_params(("arbitrary",)),
        name="page_suffix_sums" if suffix else "page_prefix_sums",
    )(x)


def _fox_sample_kernel(pt_ref, *refs, n_pages, n_steps, n_heads):
    del pt_ref
    kp = refs[:n_pages]
    vp = refs[n_pages:2 * n_pages]
    wp = refs[2 * n_pages:3 * n_pages]
    q_ref, kn_ref, vn_ref, pn_ref, o_ref, knew_scr, vnew_scr = refs[3 * n_pages:]
    page, w = kp[0].shape
    dh = w // n_heads
    rows = n_steps * n_heads

    @pl.when(pl.program_id(0) == 0)
    def _():
        knew_scr[...] = jnp.zeros_like(knew_scr)
        vnew_scr[...] = jnp.zeros_like(vnew_scr)

    knew_scr[0:SUBLANES, :] = kn_ref[...]
    vnew_scr[0:SUBLANES, :] = vn_ref[...]

    hmask = _div(_iota((n_heads, w), 1), dh) == _iota((n_heads, w), 0)
    q = q_ref[...]
    qbd = jnp.concatenate(
        [jnp.where(hmask, jnp.broadcast_to(q[s:s + 1], (n_heads, w)), 0.0) for s in range(n_steps)],
        axis=0).astype(BF16)

    scores = []
    later = jnp.zeros((n_heads, page), F32)
    last_lane = _iota((n_heads, page), 1) == page - 1
    biases = [None] * n_pages
    for j in range(n_pages - 1, -1, -1):
        winc = wp[j][...]
        excl = jnp.where(last_lane, 0.0, pltpu.roll(winc, page - 1, 1))
        biases[j] = excl + later
        later = later + jnp.broadcast_to(winc[:, 0:1], (n_heads, page))
    for j in range(n_pages):
        s = lax.dot_general(qbd, kp[j][...].astype(BF16), NT_DIMS, preferred_element_type=F32)
        scores.append(s + jnp.concatenate([biases[j]] * n_steps, axis=0))
    s_new = lax.dot_general(qbd, knew_scr[...].astype(BF16), NT_DIMS, preferred_element_type=F32)
    s_new = s_new - jnp.concatenate([pn_ref[...]] * n_steps, axis=0)
    key_pos = _iota((rows, page), 1)
    q_step = _div(_iota((rows, page), 0), n_heads)
    scores.append(jnp.where(key_pos <= q_step, s_new, NEG_BIG))

    m = scores[0].max(axis=1, keepdims=True)
    for s in scores[1:]:
        m = jnp.maximum(m, s.max(axis=1, keepdims=True))
    acc = jnp.zeros((rows, w), F32)
    den = jnp.zeros((rows, 1), F32)
    for j, s in enumerate(scores):
        p = jnp.exp(s - m)
        den = den + p.sum(axis=1, keepdims=True)
        vals = (vp[j][...] if j < n_pages else vnew_scr[...]).astype(BF16)
        acc = acc + jnp.dot(p.astype(BF16), vals, preferred_element_type=F32)
    out = acc / den
    o_ref[...] = jnp.zeros_like(o_ref)
    for s in range(n_steps):
        blk = jnp.where(hmask, out[s * n_heads:(s + 1) * n_heads], 0.0)
        o_ref[s:s + 1, :] = blk.sum(axis=0, keepdims=True)


def _fox_sample(page_table, cache_k, cache_v, winc, q8, k8, v8, pnew, *, n_steps, n_heads):
    nb, n_pages = page_table.shape
    n_phys, page, w = cache_k.shape
    pt_flat = page_table.reshape(-1)

    def page_spec(shape, j):
        return pl.BlockSpec((None,) + shape, lambda b, pt: (pt[b * n_pages + j], 0, 0))

    in_specs = ([page_spec((page, w), j) for j in range(n_pages)] * 2
                + [page_spec((n_heads, page), j) for j in range(n_pages)])
    seq_spec = pl.BlockSpec((None, SUBLANES, w), lambda b, pt: (b, 0, 0))
    in_specs += [seq_spec, seq_spec, seq_spec, pl.BlockSpec((None, n_heads, page), lambda b, pt: (b, 0, 0))]
    kern = functools.partial(_fox_sample_kernel, n_pages=n_pages, n_steps=n_steps, n_heads=n_heads)
    grid_spec = pltpu.PrefetchScalarGridSpec(
        num_scalar_prefetch=1,
        grid=(nb,),
        in_specs=in_specs,
        out_specs=seq_spec,
        scratch_shapes=[pltpu.VMEM((page, w), F32), pltpu.VMEM((page, w), F32)],
    )
    return pl.pallas_call(
        kern,
        grid_spec=grid_spec,
        out_shape=jax.ShapeDtypeStruct((nb, SUBLANES, w), F32),
        compiler_params=_params(("arbitrary",)),
        name="fox_sample",
    )(pt_flat, *([cache_k] * n_pages), *([cache_v] * n_pages), *([winc] * n_pages), q8, k8, v8, pnew)


def _outffn_kernel(*refs, sample, first_layer, d_model, n_heads, alpha):
    it = iter(refs)
    x_ref, yc_ref, oh_ref, hg_ref, yf_ref, ma_ref, mb_ref = (next(it) for _ in range(7))
    if first_layer:
        lng_ref, lnb_ref = next(it), next(it)
    ng_ref, wo_ref, wg_ref, wu_ref, wd_ref, lg_ref, lb_ref, out_ref, acc_ref = (next(it) for _ in range(9))

    tm = x_ref.shape[0]
    x = x_ref[...]
    if first_layer:
        x = _layer_norm(x, lng_ref[...], lnb_ref[...])
    ma, mb = ma_ref[...], mb_ref[...]
    if sample:
        reps = tm // ma.shape[0]
        ma = jnp.concatenate([ma] * reps, axis=0)
        mb = jnp.concatenate([mb] * reps, axis=0)
    g1, sh2 = ma[:, :d_model], ma[:, d_model:]
    sc2, g2 = mb[:, :d_model], mb[:, d_model:]

    o = oh_ref[...]
    w = o.shape[1]
    dh = w // n_heads
    bd = (_div(_iota((w, w), 0), dh) == _div(_iota((w, w), 1), dh)).astype(BF16)
    o2 = o * o
    o2_hi = o2.astype(BF16)
    o2_lo = (o2 - o2_hi.astype(F32)).astype(BF16)
    ms = (jnp.dot(o2_hi, bd, preferred_element_type=F32)
          + jnp.dot(o2_lo, bd, preferred_element_type=F32)) * (1.0 / dh)
    hg = hg_ref[...]
    yh = (o * lax.rsqrt(ms + RMS_EPS) * ng_ref[...] * (hg * jax.nn.sigmoid(hg))).astype(BF16)

    mixed = jnp.concatenate([yc_ref[...], yh, yf_ref[...]], axis=1)
    mix = jnp.dot(mixed, wo_ref[...], preferred_element_type=F32)
    x1 = _layer_norm(alpha * x + (1.0 + g1) * mix, lg_ref[0:1, :], lb_ref[0:1, :])

    h2 = (x1 * (1.0 + sc2) + sh2).astype(BF16)
    acc_ref[...] = jnp.zeros_like(acc_ref)

    def ffn_chunk(ci, carry):
        gate = jnp.dot(h2, wg_ref[ci], preferred_element_type=F32)
        up = jnp.dot(h2, wu_ref[ci], preferred_element_type=F32)
        act = (gate * jax.nn.sigmoid(gate) * up).astype(BF16)
        acc_ref[...] += jnp.dot(act, wd_ref[ci], preferred_element_type=F32)
        return carry

    lax.fori_loop(0, wg_ref.shape[0], ffn_chunk, 0)
    out_ref[...] = _layer_norm(alpha * x1 + (1.0 + g2) * acc_ref[...], lg_ref[1:2, :], lb_ref[1:2, :])


def _outffn(x, yconv, oh, hg, yfox, mod, ln_in, norm_g, w_out_b, wg, wu, wd, ln_g, ln_b,
            *, layer, n_seq, sample, alpha):
    t, d = x.shape
    tm = TOKEN_TILE
    first_layer = ln_in is not None
    depth, rows, _ = mod.shape
    tiles_per_seq = (t // n_seq) // tm if not sample else 1
    tok = lambda a: pl.BlockSpec((tm, a.shape[1]), lambda i: (i, 0))
    const = lambda a: pl.BlockSpec(a.shape, lambda i: (0,) * a.ndim, pipeline_mode=pl.Buffered(1))
    small = lambda a: pl.BlockSpec(a.shape, lambda i: (0,) * a.ndim)

    inputs = [x, yconv, oh, hg, yfox]
    in_specs = [tok(a) for a in inputs]
    if sample:
        nb = t // 4
        inputs += [mod, mod]
        in_specs += [pl.BlockSpec((None, nb, 2 * d), lambda i: (layer, 0, 1)),
                     pl.BlockSpec((None, nb, 2 * d), lambda i: (layer, 0, 2))]
    else:
        m4 = mod.reshape(depth, rows, 1, mod.shape[2])
        inputs += [m4, m4]
        row_of = lambda i: rows - SUBLANES + i // tiles_per_seq
        in_specs += [pl.BlockSpec((None, None, 1, 2 * d), lambda i: (layer, row_of(i), 0, 1)),
                     pl.BlockSpec((None, None, 1, 2 * d), lambda i: (layer, row_of(i), 0, 2))]
    if first_layer:
        inputs += [ln_in[0].reshape(1, d), ln_in[1].reshape(1, d)]
        in_specs += [pl.BlockSpec((1, d), lambda i: (0, 0))] * 2
    ng = norm_g.reshape(1, -1)
    inputs += [ng, w_out_b, wg, wu, wd, ln_g, ln_b]
    in_specs += [small(ng), const(w_out_b), const(wg), const(wu), const(wd), small(ln_g), small(ln_b)]
    kern = functools.partial(_outffn_kernel, sample=sample, first_layer=first_layer, d_model=d,
                             n_heads=HGRN_HEADS, alpha=alpha)
    return pl.pallas_call(
        kern,
        grid=(t // tm,),
        in_specs=in_specs,
        out_specs=pl.BlockSpec((tm, d), lambda i: (i, 0)),
        out_shape=jax.ShapeDtypeStruct((t, d), F32),
        scratch_shapes=[pltpu.VMEM((tm, d), F32)],
        compiler_params=_params(("arbitrary",)),
        name="outffn_sample" if sample else "outffn_prompt",
    )(*inputs)


FFN_CHUNK = 256


def kernel(x_prompt, x_sample, c_prompt, c_sample, cache_k, cache_v, cache_logf, state_hgrn, state_conv,
           page_table, ln_in_g, ln_in_b, w_mod, b_mod, w_in, conv_w, hgrn_lb_logits, hgrn_norm_g,
           fox_f_bias, w_out, w_ffn_in, w_ffn_out, ln_g, ln_b):
    nbp, seq, d = x_prompt.shape
    nbs, n_steps, _ = x_sample.shape
    depth = w_mod.shape[0]
    alpha = (2 * depth) ** 0.25
    _, n_phys, page, nh, dh = cache_k.shape
    foxw = nh * dh
    n_pages = page_table.shape[1]
    dff = w_ffn_out.shape[1]
    nchunk = dff // FFN_CHUNK

    c_all = jnp.concatenate([c_sample, c_prompt, jnp.zeros((SUBLANES - nbp, d), F32)], axis=0)
    mod = _modulation(c_all, w_mod, b_mod)

    ln_in = (ln_in_g, ln_in_b)
    xp = x_prompt.reshape(nbp * seq, d)
    xs = x_sample.transpose(1, 0, 2).reshape(n_steps * nbs, d)

    winc = _page_sums(cache_logf.reshape(depth * n_phys, page * nh), page=page, n_heads=nh, suffix=True)
    winc = winc.reshape(depth, n_phys, nh, page)

    outs_p = dict(k=[], v=[], lf=[], s=[], conv=[])
    outs_s = dict(k=[], v=[], lf=[], s=[], conv=[])
    for l in range(depth):
        w_in_p = jnp.pad(w_in[l], ((0, 0), (0, LANES - nh))).astype(BF16)
        fbias = jnp.pad(fox_f_bias[l], (0, LANES - nh)).reshape(1, LANES)
        w_out_b = w_out[l].astype(BF16)
        wg = w_ffn_in[l][:, :dff].reshape(d, nchunk, FFN_CHUNK).transpose(1, 0, 2).astype(BF16)
        wu = w_ffn_in[l][:, dff:].reshape(d, nchunk, FFN_CHUNK).transpose(1, 0, 2).astype(BF16)
        wd = w_ffn_out[l].reshape(nchunk, FFN_CHUNK, d).astype(BF16)
        lin = ln_in if l == 0 else None

        (yconv, u, qh, kh, lfh, vh, hg, fq, fk, fv, fkb, fvb, lff, c, ct) = _inproj(
            xp, mod, lin, w_in_p, conv_w[l], hgrn_lb_logits, fbias, None, layer=l, n_seq=nbp, sample=False)
        oh, st_t = _hgrn_prompt(qh, kh, lfh, vh, n_seq=nbp)
        yfox = _fox_prompt(fq, fkb, fvb, c, ct, n_seq=nbp)
        xp = _outffn(xp, yconv, oh, hg, yfox, mod, lin, hgrn_norm_g[l], w_out_b, wg, wu, wd,
                     ln_g[l], ln_b[l], layer=l, n_seq=nbp, sample=False, alpha=alpha)
        hgw = qh.shape[1]
        hdk = hgw // HGRN_HEADS
        st4 = st_t.reshape(nbp, HGRN_HEADS, hdk, HGRN_HEADS, hdk)
        outs_p["s"].append(jnp.stack([st4[:, hh, :, hh, :] for hh in range(HGRN_HEADS)], axis=1))
        outs_p["k"].append(fk.reshape(nbp, seq, nh, dh))
        outs_p["v"].append(fv.reshape(nbp, seq, nh, dh))
        outs_p["lf"].append(lff.reshape(nbp, seq, nh))
        outs_p["conv"].append(u.reshape(nbp, seq, -1)[:, seq - (CONV_K - 1):, :])

        prev = state_conv[l].transpose(1, 0, 2).reshape((CONV_K - 1) * nbs, -1)
        (yconv, u, qh, kh, lfh, vh, hg, fq, fk, fv, _, _, lff) = _inproj(
            xs, mod, lin, w_in_p, conv_w[l], hgrn_lb_logits, fbias, prev, layer=l, n_seq=nbs, sample=True)
        s0 = state_hgrn[l].reshape(nbs, -1)
        oh, s_fin = _hgrn_sample(qh, kh, lfh, vh, s0, n_steps=n_steps)

        by_seq = lambda a: a.reshape(n_steps, nbs, -1).transpose(1, 0, 2)
        pad8 = lambda a: jnp.pad(a, ((0, 0), (0, SUBLANES - n_steps), (0, 0)))
        k_bs, v_bs, lf_bs = by_seq(fk), by_seq(fv), by_seq(lff)
        lf_pages = jnp.pad(lf_bs.reshape(nbs, n_steps * nh), ((0, 0), (0, (page - n_steps) * nh)))
        pnew = _page_sums(lf_pages, page=page, n_heads=nh, suffix=False).reshape(nbs, nh, page)
        y8 = _fox_sample(page_table, cache_k[l].reshape(n_phys, page, foxw),
                         cache_v[l].reshape(n_phys, page, foxw), winc[l],
                         pad8(by_seq(fq.astype(F32))), pad8(k_bs), pad8(v_bs), pnew,
                         n_steps=n_steps, n_heads=nh)
        yfox = y8[:, :n_steps, :].transpose(1, 0, 2).reshape(n_steps * nbs, foxw).astype(BF16)
        xs = _outffn(xs, yconv, oh, hg, yfox, mod, lin, hgrn_norm_g[l], w_out_b, wg, wu, wd,
                     ln_g[l], ln_b[l], layer=l, n_seq=nbs, sample=True, alpha=alpha)
        outs_s["s"].append(s_fin.reshape(nbs, HGRN_HEADS, hdk, hdk))
        outs_s["k"].append(k_bs.reshape(nbs, n_steps, nh, dh))
        outs_s["v"].append(v_bs.reshape(nbs, n_steps, nh, dh))
        outs_s["lf"].append(lf_bs)
        outs_s["conv"].append(by_seq(u)[:, n_steps - (CONV_K - 1):, :])

    y_prompt = xp.reshape(nbp, seq, d)
    y_sample = xs.reshape(n_steps, nbs, d).transpose(1, 0, 2)
    st = lambda xs_: jnp.stack(xs_)
    return (y_prompt, y_sample,
            st(outs_p["k"]), st(outs_p["v"]), st(outs_p["lf"]), st(outs_p["s"]), st(outs_p["conv"]),
            st(outs_s["k"]), st(outs_s["v"]), st(outs_s["lf"]), st(outs_s["s"]), st(outs_s["conv"]))
```

```python
import functools

import jax
import jax.numpy as jnp
from jax import lax
from jax.experimental import pallas as pl
from jax.experimental.pallas import tpu as pltpu

F32 = jnp.float32
BF16 = jnp.bfloat16

HEAD_DIM = 64
HGRN_HEADS = 4
CONV_K = 3
LN_EPS = 1e-5
RMS_EPS = 1e-6
NEG_BIG = -1e30
LOG2E = 1.4426950408889634

LANES = 128
SUBLANES = 8
VMEM_LIMIT = 56 * 1024 * 1024
TOKEN_TILE = 512
HGRN_CHUNK = 128
MOD_TILE_N = 1536
FFN_CHUNK = 256
N_BIAS_PARTS = 3

NT_DIMS = (((1,), (1,)), ((), ()))
TN_DIMS = (((0,), (0,)), ((), ()))


def _params(semantics):
    return pltpu.CompilerParams(dimension_semantics=semantics, vmem_limit_bytes=VMEM_LIMIT)


def _layer_norm(x, g, b):
    mu = jnp.mean(x, axis=-1, keepdims=True)
    xc = x - mu
    var = jnp.mean(xc * xc, axis=-1, keepdims=True)
    return xc * lax.rsqrt(var + LN_EPS) * g + b


def _log_sigmoid(x):
    return -(jnp.maximum(-x, 0.0) + jnp.log1p(jnp.exp(-jnp.abs(x))))


def _split3(x):
    hi = x.astype(BF16)
    r = x - hi.astype(F32)
    mid = r.astype(BF16)
    lo = (r - mid.astype(F32)).astype(BF16)
    return hi, mid, lo


def _dot_exact_lhs(a, parts):
    return sum(jnp.dot(a, p, preferred_element_type=F32) for p in parts)


def _dot_exact_rhs(parts, a):
    return sum(jnp.dot(p, a, preferred_element_type=F32) for p in parts)


def _iota(shape, axis):
    return lax.broadcasted_iota(jnp.int32, shape, axis)


def _log2(n):
    assert n > 0 and n & (n - 1) == 0, n
    return n.bit_length() - 1


def _div(x, n):
    return lax.shift_right_logical(x, _log2(n))


def _mod(x, n):
    return jnp.bitwise_and(x, n - 1)


def _mod_kernel(c_ref, w_ref, b_ref, o_ref):
    c = c_ref[...]
    a = (c * jax.nn.sigmoid(c)).astype(BF16)
    o_ref[...] = jnp.dot(a, w_ref[...].astype(BF16), preferred_element_type=F32) + b_ref[...]


def _modulation(c_all, w_mod, b_mod):
    depth, d, n = w_mod.shape
    rows = c_all.shape[0]
    return pl.pallas_call(
        _mod_kernel,
        grid=(depth, n // MOD_TILE_N),
        in_specs=[
            pl.BlockSpec((rows, d), lambda l, j: (0, 0)),
            pl.BlockSpec((None, d, MOD_TILE_N), lambda l, j: (l, 0, j)),
            pl.BlockSpec((None, 1, MOD_TILE_N), lambda l, j: (l, 0, j)),
        ],
        out_specs=pl.BlockSpec((None, rows, MOD_TILE_N), lambda l, j: (l, 0, j)),
        out_shape=jax.ShapeDtypeStruct((depth, rows, n), F32),
        compiler_params=_params(("arbitrary", "arbitrary")),
        name="modulation",
    )(c_all, w_mod, b_mod.reshape(depth, 1, n))


def _head_rows(x, hh, dh):
    per = LANES // dh
    slab = x[:, (hh // per) * LANES:(hh // per + 1) * LANES]
    shift = (hh % per) * dh
    return pltpu.roll(slab, LANES - shift, 1) if shift else slab


def _inproj_kernel(*refs, layer, sample, first_layer, tiles_per_seq, n_steps, d_model, conv_w, hgrn_w, fox_w):
    it = iter(refs)
    x_ref, mod_ref = next(it), next(it)
    if first_layer:
        lng_ref, lnb_ref = next(it), next(it)
    w_ref, cw_ref, lbl_ref, fb_ref = next(it), next(it), next(it), next(it)
    if sample:
        prev_ref = next(it)
        (yconv_ref, u_ref, qh_ref, kh_ref, lfh_ref, vh_ref, hg_ref,
         fq_ref, fk_ref, fv_ref, fkt_ref, fvt_ref, lfft_ref) = (next(it) for _ in range(13))
    else:
        (yconv_ref, ulast_ref, qh_ref, kh_ref, lfh_ref, vh_ref, hg_ref,
         qa_ref, kat_ref, va_ref, fkt_ref, fvt_ref, lfft_ref, cbase_ref,
         carry_ref, ccarry_ref) = (next(it) for _ in range(16))

    tm = x_ref.shape[0]
    dh = HEAD_DIM
    nh = fox_w // dh
    x = x_ref[...]
    if first_layer:
        x = _layer_norm(x, lng_ref[...], lnb_ref[...])
    m = mod_ref[...]
    if sample:
        m = jnp.concatenate([m] * (tm // m.shape[0]), axis=0)
    sh1, sc1 = m[:, :d_model], m[:, d_model:]
    h = (x * (1.0 + sc1) + sh1).astype(BF16)

    c0 = 0
    pc = jnp.dot(h, w_ref[:, c0:c0 + 3 * conv_w], preferred_element_type=F32)
    cb, cc, ch = pc[:, :conv_w], pc[:, conv_w:2 * conv_w], pc[:, 2 * conv_w:]
    u = cc * ch
    if sample:
        nb = tm // n_steps
        prev0, prev1 = prev_ref[:nb, :], prev_ref[nb:, :]
        u1 = jnp.concatenate([prev1, u[:tm - nb]], axis=0)
        u2 = jnp.concatenate([prev0, prev1, u[:tm - 2 * nb]], axis=0)
        u_ref[...] = u
    else:
        i = pl.program_id(0)

        @pl.when(i % tiles_per_seq == 0)
        def _():
            carry_ref[...] = jnp.zeros_like(carry_ref)
            ccarry_ref[...] = jnp.zeros_like(ccarry_ref)

        prev = carry_ref[...]
        row = _iota((tm, 1), 0)
        last1, last2 = prev[SUBLANES - 1:SUBLANES], prev[SUBLANES - 2:SUBLANES - 1]
        u1 = jnp.where(row == 0, last1, pltpu.roll(u, 1, 0))
        u2 = jnp.where(row == 0, last2, jnp.where(row == 1, last1, pltpu.roll(u, 2, 0)))
        carry_ref[...] = u[tm - SUBLANES:]
        ulast_ref[...] = u[tm - SUBLANES:]
    cw = cw_ref[...]
    yconv_ref[...] = (cb * (cw[0:1] * u2 + cw[1:2] * u1 + cw[2:3] * u)).astype(BF16)

    c0 += 3 * conv_w
    ph = jnp.dot(h, w_ref[:, c0:c0 + 4 * hgrn_w], preferred_element_type=F32)
    hq, hf = ph[:, :hgrn_w], ph[:, hgrn_w:2 * hgrn_w]
    hi, hg = ph[:, 2 * hgrn_w:3 * hgrn_w], ph[:, 3 * hgrn_w:]
    logits = lbl_ref[...]
    e = jnp.exp(logits - jnp.max(logits, axis=0, keepdims=True))
    p = e / jnp.sum(e, axis=0, keepdims=True)
    csum = p[0:1]
    for j in range(1, layer + 1):
        csum = csum + p[j:j + 1]
    lb = csum - p[layer:layer + 1]
    has_lb = lb > 0
    log_lb = jnp.where(has_lb, jnp.log(jnp.where(has_lb, lb, 1.0)), NEG_BIG)
    b2 = jnp.log1p(-lb) + _log_sigmoid(hf)
    lfh = jnp.maximum(log_lb, b2) + jnp.log1p(jnp.exp(-jnp.abs(log_lb - b2)))
    kh = (1.0 - lb) * jax.nn.sigmoid(-hf)
    qh = hq * jax.nn.sigmoid(hq)
    hg_ref[...] = hg
    if sample:
        for s in range(n_steps):
            rows = slice(s * nb, (s + 1) * nb)
            qh_ref[s] = qh[rows].T
            kh_ref[s] = kh[rows].T
            lfh_ref[s] = lfh[rows].T
            vh_ref[s] = hi[rows].T
    else:
        qh_ref[...] = qh
        kh_ref[...] = kh
        lfh_ref[...] = lfh
        vh_ref[...] = hi

    c0 += 4 * hgrn_w
    pf = jnp.dot(h, w_ref[:, c0:c0 + 3 * fox_w], preferred_element_type=F32)
    fq, fk, fv = pf[:, :fox_w], pf[:, fox_w:2 * fox_w], pf[:, 2 * fox_w:]
    c0 += 3 * fox_w
    pff = jnp.dot(h, w_ref[:, c0:c0 + LANES], preferred_element_type=F32)
    lff = _log_sigmoid(pff + fb_ref[...])
    if sample:
        fq_ref[...] = (fq * (dh ** -0.5)).reshape(fq_ref.shape)
        fk_ref[...] = fk.reshape(fk_ref.shape)
        fv_ref[...] = fv.reshape(fv_ref.shape)
        for s in range(n_steps):
            rows = slice(s * nb, (s + 1) * nb)
            fkt_ref[s] = fk[rows].T
            fvt_ref[s] = fv[rows].T
            lfft_ref[s] = lff[rows].T[:nh]
        return

    fkt = fk.T
    fkt_ref[...] = fkt
    fvt_ref[...] = fv.T
    lfft_ref[...] = lff.T[:nh]
    tri = (_iota((tm, tm), 0) >= _iota((tm, tm), 1)).astype(BF16)
    c_nat = _dot_exact_lhs(tri, _split3(lff))
    c_rel = c_nat * LOG2E
    c_rel_t = c_rel.T
    base = ccarry_ref[...]
    cbase_ref[...] = base * LOG2E
    ccarry_ref[...] = base + jnp.broadcast_to(c_nat[tm - 1:tm], ccarry_ref.shape)

    lane = _iota((tm, LANES), 1)
    srow = _iota((SUBLANES, tm), 0)
    npz = N_BIAS_PARTS
    for hh in range(nh):
        cq = [pp.astype(F32) for pp in _split3(jnp.broadcast_to(c_rel[:, hh:hh + 1], (tm, LANES)))]
        extra = jnp.where(lane < dh + 2 * npz, 1.0, 0.0)
        for j in range(npz):
            extra = jnp.where(lane == dh + j, cq[j], extra)
        q_slab = _head_rows(fq, hh, dh) * (LOG2E * dh ** -0.5)
        qa_ref[:, hh * LANES:(hh + 1) * LANES] = jnp.where(lane < dh, q_slab, extra).astype(BF16)
        v_slab = _head_rows(fv, hh, dh)
        va_ref[:, hh * LANES:(hh + 1) * LANES] = jnp.where(
            lane < dh, v_slab, jnp.where(lane == dh, 1.0, 0.0)).astype(BF16)
        ck = [pp.astype(F32) for pp in _split3(c_rel_t[hh:hh + 1, :])]
        grp = jnp.where(srow < npz, 1.0, 0.0)
        for j in range(npz):
            grp = jnp.where(srow == npz + j, -ck[j], grp)
        kat_ref[hh * LANES:(hh + 1) * LANES, :] = jnp.concatenate(
            [fkt[hh * dh:(hh + 1) * dh], grp, jnp.zeros((LANES - dh - SUBLANES, tm), F32)], axis=0).astype(BF16)


def _inproj(x, mod, ln_in, w_in_p, conv_w, lb_logits, fbias, prev, *, layer, n_seq, sample):
    t, d = x.shape
    tm = TOKEN_TILE
    n_tiles = t // tm
    seq = t // n_seq
    tiles_per_seq = seq // tm if not sample else 1
    n_steps = t // n_seq if sample else 0
    convw = conv_w.shape[1]
    hgw = lb_logits.shape[1]
    ncols = w_in_p.shape[1]
    foxw = (ncols - LANES - 3 * convw - 4 * hgw) // 3
    nh = foxw // HEAD_DIM
    first_layer = ln_in is not None
    depth = mod.shape[0]

    inputs, in_specs = [x], [pl.BlockSpec((tm, d), lambda i: (i, 0))]
    if sample:
        assert n_tiles == 1
        inputs.append(mod)
        in_specs.append(pl.BlockSpec((None, n_seq, 2 * d), lambda i: (layer, 0, 0)))
    else:
        rows = mod.shape[1]
        inputs.append(mod.reshape(depth, rows, 1, mod.shape[2]))
        in_specs.append(pl.BlockSpec((None, None, 1, 2 * d),
                                     lambda i: (layer, rows - SUBLANES + i // tiles_per_seq, 0, 0)))
    if first_layer:
        inputs += [ln_in[0].reshape(1, d), ln_in[1].reshape(1, d)]
        in_specs += [pl.BlockSpec((1, d), lambda i: (0, 0))] * 2
    inputs += [w_in_p, conv_w, lb_logits, fbias]
    in_specs += [
        pl.BlockSpec((d, ncols), lambda i: (0, 0), pipeline_mode=pl.Buffered(1)),
        pl.BlockSpec(conv_w.shape, lambda i: (0, 0)),
        pl.BlockSpec(lb_logits.shape, lambda i: (0, 0)),
        pl.BlockSpec((1, LANES), lambda i: (0, 0)),
    ]
    tok = lambda w, dt: (jax.ShapeDtypeStruct((t, w), dt), pl.BlockSpec((tm, w), lambda i: (i, 0)))
    whole = lambda shape, dt: (jax.ShapeDtypeStruct(shape, dt), pl.BlockSpec(shape, lambda i: (0,) * len(shape)))
    scratch = []
    if sample:
        inputs.append(prev)
        in_specs.append(pl.BlockSpec(prev.shape, lambda i: (0, 0)))
        outs = ([tok(convw, BF16), tok(convw, F32)] + [whole((n_steps, hgw, n_seq), F32)] * 4 + [tok(hgw, F32)]
                + [whole((n_steps, n_seq, foxw), F32)] * 3 + [whole((n_steps, foxw, n_seq), F32)] * 2
                + [whole((n_steps, nh, n_seq), F32)])
    else:
        per_seq_t = lambda rows, dt: (
            jax.ShapeDtypeStruct((n_seq, rows, seq), dt),
            pl.BlockSpec((None, rows, tm), lambda i: (i // tiles_per_seq, 0, i % tiles_per_seq)))
        outs = ([tok(convw, BF16),
                 (jax.ShapeDtypeStruct((n_seq, SUBLANES, convw), F32),
                  pl.BlockSpec((None, SUBLANES, convw), lambda i: (i // tiles_per_seq, 0, 0)))]
                + [tok(hgw, F32)] * 5
                + [tok(nh * LANES, BF16), per_seq_t(nh * LANES, BF16), tok(nh * LANES, BF16),
                   per_seq_t(foxw, F32), per_seq_t(foxw, F32), per_seq_t(nh, F32),
                   (jax.ShapeDtypeStruct((n_tiles, SUBLANES, LANES), F32),
                    pl.BlockSpec((None, SUBLANES, LANES), lambda i: (i, 0, 0)))])
        scratch = [pltpu.VMEM((SUBLANES, convw), F32), pltpu.VMEM((SUBLANES, LANES), F32)]
    kern = functools.partial(_inproj_kernel, layer=layer, sample=sample, first_layer=first_layer,
                             tiles_per_seq=tiles_per_seq, n_steps=n_steps, d_model=d, conv_w=convw,
                             hgrn_w=hgw, fox_w=foxw)
    return pl.pallas_call(
        kern,
        grid=(n_tiles,),
        in_specs=in_specs,
        out_specs=[o[1] for o in outs],
        out_shape=[o[0] for o in outs],
        scratch_shapes=scratch,
        compiler_params=_params(("arbitrary",)),
        name="inproj_sample" if sample else "inproj_prompt",
    )(*inputs)


def _hgrn_prompt_kernel(q_ref, k_ref, g_ref, v_ref, o_ref, st_out_ref, st_ref, *, chunk, n_heads):
    i = pl.program_id(1)
    tm, w = q_ref.shape
    dh = w // n_heads
    c = chunk

    @pl.when(i == 0)
    def _():
        st_ref[...] = jnp.zeros_like(st_ref)

    r_cc, c_cc = _iota((c, c), 0), _iota((c, c), 1)
    tri = (r_cc >= c_cc).astype(BF16)
    lane = _iota((1, w), 1)
    head_masks = [_div(lane, dh) == hh for hh in range(n_heads)]
    bd = _div(_iota((w, w), 0), dh) == _div(_iota((w, w), 1), dh)
    bd_bf = bd.astype(BF16)
    row = _iota((c, 1), 0)
    col_key = _mod(_iota((c, n_heads * c), 1), c)
    row_q = _iota((c, n_heads * c), 0)
    levels = []
    b = c // 2
    while b >= 1:
        ref_row = _div(r_cc, 2 * b) * (2 * b) + (b - 1)
        levels.append(dict(
            sel=(c_cc == ref_row).astype(BF16),
            upper=_mod(row, 2 * b) >= b,
            lower=_mod(row, 2 * b) < b,
            same=_div(row_q, 2 * b) == _div(col_key, 2 * b),
        ))
        b //= 2

    def chunk_body(ci, carry):
        r0 = pl.multiple_of(ci * c, c)
        q = q_ref[pl.ds(r0, c), :]
        k = k_ref[pl.ds(r0, c), :]
        g = g_ref[pl.ds(r0, c), :]
        v = v_ref[pl.ds(r0, c), :]
        vb = v.astype(BF16)
        gcum = _dot_exact_lhs(tri, _split3(g))
        gparts = _split3(gcum)
        a = jnp.zeros((c, n_heads * c), F32)
        for lv in levels:
            gref = _dot_exact_lhs(lv["sel"], gparts)
            eq = jnp.exp(jnp.where(lv["upper"], gcum - gref, NEG_BIG))
            ek = jnp.exp(jnp.where(lv["lower"], gref - gcum, NEG_BIG))
            qt = (q * eq).astype(BF16)
            kt = (k * ek).astype(BF16)
            kbd = jnp.concatenate([jnp.where(hm, kt, jnp.zeros_like(kt)) for hm in head_masks], axis=0)
            xl = lax.dot_general(qt, kbd, NT_DIMS, preferred_element_type=F32)
            a = jnp.where(lv["same"], xl, a)
        qk_hi = (q * k).astype(BF16)
        qk_lo = (q * k - qk_hi.astype(F32)).astype(BF16)
        diag = (jnp.dot(qk_hi, bd_bf, preferred_element_type=F32)
                + jnp.dot(qk_lo, bd_bf, preferred_element_type=F32))
        o = diag * v
        vbd = jnp.concatenate([jnp.where(hm, vb, jnp.zeros_like(vb)) for hm in head_masks], axis=0)
        o = o + jnp.dot(a.astype(BF16), vbd, preferred_element_type=F32)
        st = st_ref[...]
        qg = (q * jnp.exp(gcum)).astype(BF16)
        o = o + lax.dot_general(qg, st.astype(BF16), NT_DIMS, preferred_element_type=F32)
        o_ref[pl.ds(r0, c), :] = o
        g_last = gcum[c - 1:c]
        kd = (k * jnp.exp(g_last - gcum)).astype(BF16)
        upd = lax.dot_general(vb, kd, TN_DIMS, preferred_element_type=F32)
        st_ref[...] = st * jnp.exp(g_last) + jnp.where(bd, upd, 0.0)
        return carry

    lax.fori_loop(0, tm // c, chunk_body, 0)

    @pl.when(i == pl.num_programs(1) - 1)
    def _():
        st_out_ref[...] = st_ref[...].T


def _hgrn_prompt(qh, kh, lfh, vh, *, n_seq):
    t, w = qh.shape
    tm = TOKEN_TILE
    tps = (t // n_seq) // tm
    spec = pl.BlockSpec((tm, w), lambda b, i: (b * tps + i, 0))
    kern = functools.partial(_hgrn_prompt_kernel, chunk=HGRN_CHUNK, n_heads=HGRN_HEADS)
    return pl.pallas_call(
        kern,
        grid=(n_seq, tps),
        in_specs=[spec] * 4,
        out_specs=[spec, pl.BlockSpec((None, w, w), lambda b, i: (b, 0, 0))],
        out_shape=[jax.ShapeDtypeStruct((t, w), F32), jax.ShapeDtypeStruct((n_seq, w, w), F32)],
        scratch_shapes=[pltpu.VMEM((w, w), F32)],
        compiler_params=_params(("arbitrary", "arbitrary")),
        name="hgrn_prompt",
    )(qh, kh, lfh, vh)


def _hgrn_sample_kernel(q_ref, k_ref, g_ref, v_ref, s0_ref, o_ref, s_out_ref):
    n_steps, dh, nb = q_ref.shape

    def body(d, o_acc):
        s = s0_ref[d]
        out = []
        for t in range(n_steps):
            row = pl.ds(d, 1)
            s = jnp.exp(g_ref[t, row, :]) * s + k_ref[t, row, :] * v_ref[t]
            out.append(o_acc[t] + q_ref[t, row, :] * s)
        s_out_ref[d] = s
        return tuple(out)

    o = lax.fori_loop(0, dh, body, tuple(jnp.zeros((v_ref.shape[1], nb), F32) for _ in range(n_steps)))
    for t in range(n_steps):
        o_ref[t] = o[t]


def _hgrn_sample(qh_t, kh_t, lfh_t, vh_t, state_t, *, layer):
    n_steps, w, nb = qh_t.shape
    _, n_heads, dk, dv, _ = state_t.shape
    gate = pl.BlockSpec((n_steps, dk, nb), lambda h: (0, h, 0))
    val = pl.BlockSpec((n_steps, dv, nb), lambda h: (0, h, 0))
    return pl.pallas_call(
        _hgrn_sample_kernel,
        grid=(n_heads,),
        in_specs=[gate, gate, gate, val,
                  pl.BlockSpec((None, None, dk, dv, nb), lambda h: (layer, h, 0, 0, 0))],
        out_specs=[val, pl.BlockSpec((None, dk, dv, nb), lambda h: (h, 0, 0, 0))],
        out_shape=[jax.ShapeDtypeStruct((n_steps, n_heads * dv, nb), F32),
                   jax.ShapeDtypeStruct((n_heads, dk, dv, nb), F32)],
        compiler_params=_params(("arbitrary",)),
        name="hgrn_sample",
    )(qh_t, kh_t, lfh_t, vh_t, state_t)


def _fox_prompt_kernel(qi_ref, kj_ref, qa_ref, kat_ref, va_ref, cbq_ref, cbk_ref, o_ref, m_scr, acc_scr):
    pair = pl.program_id(1)
    qi, kj = qi_ref[pair], kj_ref[pair]
    tq, tk = qa_ref.shape[0], va_ref.shape[0]
    nh = m_scr.shape[0]
    dh = HEAD_DIM

    @pl.when(kj == 0)
    def _():
        m_scr[...] = jnp.full_like(m_scr, NEG_BIG)
        acc_scr[...] = jnp.zeros_like(acc_scr)

    delta = cbq_ref[0:1, :] - cbk_ref[0:1, :]

    def tile(masked):
        if masked:
            keep = _iota((tq, tk), 1) <= _iota((tq, tk), 0)
        for hh in range(nh):
            ls = slice(hh * LANES, (hh + 1) * LANES)
            s = jnp.dot(qa_ref[:, ls], kat_ref[ls, :], preferred_element_type=F32)
            if masked:
                s = jnp.where(keep, s, NEG_BIG)
            d = delta[:, hh:hh + 1]
            m_prev = m_scr[hh]
            m_next = jnp.maximum(m_prev, jnp.max(s, axis=1, keepdims=True) + d)
            shift = m_next - d
            p = jnp.exp2(s - jnp.concatenate([shift] * (tk // LANES), axis=1))
            alpha = jnp.exp2(m_prev - m_next)
            pv = jnp.dot(p.astype(BF16), va_ref[:, ls], preferred_element_type=F32)
            acc_scr[hh] = acc_scr[hh] * alpha + pv
            m_scr[hh] = m_next

    pl.when(kj < qi)(lambda: tile(False))
    pl.when(kj == qi)(lambda: tile(True))

    @pl.when(kj == qi)
    def _():
        lane = _iota((tq, LANES), 1)
        per = LANES // dh
        for slab in range(nh // per):
            out = jnp.zeros((tq, LANES), F32)
            for j in range(per):
                acc = acc_scr[slab * per + j]
                val = acc / acc[:, dh:dh + 1]
                if j:
                    val = pltpu.roll(val, j * dh, 1)
                out = jnp.where(_div(lane, dh) == j, val, out)
            o_ref[:, slab * LANES:(slab + 1) * LANES] = out.astype(BF16)


def _fox_prompt(qa, kat, va, cbase, *, n_seq, n_heads):
    t, wa = qa.shape
    tq = tk = TOKEN_TILE
    nq = (t // n_seq) // tq
    pairs = [(i, j) for i in range(nq) for j in range(i + 1)]
    qi_tab = jnp.asarray([p[0] for p in pairs], jnp.int32)
    kj_tab = jnp.asarray([p[1] for p in pairs], jnp.int32)
    grid_spec = pltpu.PrefetchScalarGridSpec(
        num_scalar_prefetch=2,
        grid=(n_seq, len(pairs)),
        in_specs=[
            pl.BlockSpec((tq, wa), lambda b, p, qi, kj: (b * nq + qi[p], 0)),
            pl.BlockSpec((None, wa, tk), lambda b, p, qi, kj: (b, 0, kj[p])),
            pl.BlockSpec((tk, wa), lambda b, p, qi, kj: (b * nq + kj[p], 0)),
            pl.BlockSpec((None, SUBLANES, LANES), lambda b, p, qi, kj: (b * nq + qi[p], 0, 0)),
            pl.BlockSpec((None, SUBLANES, LANES), lambda b, p, qi, kj: (b * nq + kj[p], 0, 0)),
        ],
        out_specs=pl.BlockSpec((tq, n_heads * HEAD_DIM), lambda b, p, qi, kj: (b * nq + qi[p], 0)),
        scratch_shapes=[pltpu.VMEM((n_heads, tq, LANES), F32), pltpu.VMEM((n_heads, tq, LANES), F32)],
    )
    return pl.pallas_call(
        _fox_prompt_kernel,
        grid_spec=grid_spec,
        out_shape=jax.ShapeDtypeStruct((t, n_heads * HEAD_DIM), BF16),
        compiler_params=_params(("arbitrary", "arbitrary")),
        name="fox_prompt",
    )(qi_tab, kj_tab, qa, kat, va, cbase, cbase)


def _fox_sample_kernel(pt_ref, *refs, n_pages, n_steps, n_heads):
    del pt_ref
    kp = refs[:n_pages]
    vp = refs[n_pages:2 * n_pages]
    lp = refs[2 * n_pages:3 * n_pages]
    q_ref, kn_ref, vn_ref, lfn_ref, o_ref, knew_scr, vnew_scr = refs[3 * n_pages:]
    _, dh, page = kp[0].shape
    w = n_heads * dh
    rows = n_steps * n_heads
    b = pl.program_id(0)
    r = pl.ds(_mod(b, SUBLANES), 1)

    @pl.when(b == 0)
    def _():
        knew_scr[...] = jnp.zeros_like(knew_scr)
        vnew_scr[...] = jnp.zeros_like(vnew_scr)

    hmask = _div(_iota((n_heads, w), 1), dh) == _iota((n_heads, w), 0)
    qbd = []
    for s in range(n_steps):
        knew_scr[s:s + 1, :] = kn_ref[s, r, :]
        vnew_scr[s:s + 1, :] = vn_ref[s, r, :]
        qbd.append(jnp.where(hmask, jnp.broadcast_to(q_ref[s, r, :], (n_heads, w)), 0.0))
    qbd = jnp.concatenate(qbd, axis=0).astype(BF16)

    lf_all = jnp.concatenate([lp[j][...] for j in range(n_pages)], axis=0)
    parts = _split3(lf_all)
    after = (_iota((page, page), 0) > _iota((page, page), 1)).astype(BF16)
    excl = _dot_exact_rhs(parts, after)
    total = _dot_exact_rhs(parts, jnp.ones((page, page), BF16))
    scores = [None] * n_pages
    later = jnp.zeros((n_heads, page), F32)
    for j in range(n_pages - 1, -1, -1):
        ps = slice(j * n_heads, (j + 1) * n_heads)
        bias = excl[ps] + later
        later = later + total[ps]
        kt = kp[j][...].reshape(w, page).astype(BF16)
        s = jnp.dot(qbd, kt, preferred_element_type=F32)
        scores[j] = s + jnp.concatenate([bias] * n_steps, axis=0)
    lane_h = _iota((n_heads, page), 1)
    seq_lane = _iota(lfn_ref.shape[1:], 1) == b
    bias_new = jnp.zeros((n_heads, page), F32)
    cum = jnp.zeros((n_heads, 1), F32)
    for s in range(n_steps):
        cum = cum + jnp.sum(jnp.where(seq_lane, lfn_ref[s], 0.0), axis=1, keepdims=True)
        bias_new = jnp.where(lane_h == s, -cum, bias_new)
    s_new = lax.dot_general(qbd, knew_scr[...].astype(BF16), NT_DIMS, preferred_element_type=F32)
    s_new = s_new + jnp.concatenate([bias_new] * n_steps, axis=0)
    key_pos = _iota((rows, page), 1)
    q_step = _div(_iota((rows, page), 0), n_heads)
    scores.append(jnp.where(key_pos <= q_step, s_new, NEG_BIG))

    m = scores[0].max(axis=1, keepdims=True)
    for s in scores[1:]:
        m = jnp.maximum(m, s.max(axis=1, keepdims=True))
    acc = jnp.zeros((rows, w), F32)
    den = jnp.zeros((rows, 1), F32)
    for j, s in enumerate(scores):
        p = jnp.exp(s - m)
        den = den + p.sum(axis=1, keepdims=True)
        pb = p.astype(BF16)
        if j < n_pages:
            vt = vp[j][...].reshape(w, page).astype(BF16)
            acc = acc + lax.dot_general(pb, vt, NT_DIMS, preferred_element_type=F32)
        else:
            acc = acc + jnp.dot(pb, vnew_scr[...].astype(BF16), preferred_element_type=F32)
    out = acc / den
    for s in range(n_steps):
        blk = jnp.where(hmask, out[s * n_heads:(s + 1) * n_heads], 0.0)
        o_ref[s, r, :] = blk.sum(axis=0, keepdims=True)


def _fox_sample(page_table, ck_t, cv_t, clf_t, fq, fk, fv, lff_t, *, layer):
    nb, n_pages = page_table.shape
    _, n_phys, n_heads, dh, page = ck_t.shape
    n_steps = fq.shape[0]
    w = n_heads * dh
    pt_flat = page_table.reshape(-1)

    def page_spec(shape, j):
        nd = len(shape)
        return pl.BlockSpec((None, None) + shape, lambda b, pt: (layer, pt[b * n_pages + j]) + (0,) * nd)

    in_specs = ([page_spec((n_heads, dh, page), j) for j in range(n_pages)] * 2
                + [page_spec((n_heads, page), j) for j in range(n_pages)])
    step_spec = pl.BlockSpec((n_steps, SUBLANES, w), lambda b, pt: (0, b // SUBLANES, 0))
    in_specs += [step_spec] * 3 + [pl.BlockSpec(lff_t.shape, lambda b, pt: (0, 0, 0))]
    kern = functools.partial(_fox_sample_kernel, n_pages=n_pages, n_steps=n_steps, n_heads=n_heads)
    grid_spec = pltpu.PrefetchScalarGridSpec(
        num_scalar_prefetch=1,
        grid=(nb,),
        in_specs=in_specs,
        out_specs=step_spec,
        scratch_shapes=[pltpu.VMEM((page, w), F32), pltpu.VMEM((page, w), F32)],
    )
    return pl.pallas_call(
        kern,
        grid_spec=grid_spec,
        out_shape=jax.ShapeDtypeStruct((n_steps, nb, w), F32),
        compiler_params=_params(("arbitrary",)),
        name="fox_sample",
    )(pt_flat, *([ck_t] * n_pages), *([cv_t] * n_pages), *([clf_t] * n_pages), fq, fk, fv, lff_t)


def _outffn_kernel(*refs, sample, first_layer, d_model, n_heads, alpha):
    it = iter(refs)
    x_ref, yc_ref, oh_ref, hg_ref, yf_ref, ma_ref, mb_ref = (next(it) for _ in range(7))
    if first_layer:
        lng_ref, lnb_ref = next(it), next(it)
    ng_ref, wo_ref, wg_ref, wu_ref, wd_ref, lg_ref, lb_ref, out_ref, acc_ref = (next(it) for _ in range(9))

    tm = x_ref.shape[0]
    x = x_ref[...]
    if first_layer:
        x = _layer_norm(x, lng_ref[...], lnb_ref[...])
    ma, mb = ma_ref[...], mb_ref[...]
    if sample:
        reps = tm // ma.shape[0]
        ma = jnp.concatenate([ma] * reps, axis=0)
        mb = jnp.concatenate([mb] * reps, axis=0)
        o = jnp.concatenate([oh_ref[s].T for s in range(oh_ref.shape[0])], axis=0)
        yf = yf_ref[...].reshape(tm, yf_ref.shape[2]).astype(BF16)
    else:
        o = oh_ref[...]
        yf = yf_ref[...]
    g1, sh2 = ma[:, :d_model], ma[:, d_model:]
    sc2, g2 = mb[:, :d_model], mb[:, d_model:]

    w = o.shape[1]
    dh = w // n_heads
    bd = (_div(_iota((w, w), 0), dh) == _div(_iota((w, w), 1), dh)).astype(BF16)
    o2 = o * o
    o2_hi = o2.astype(BF16)
    o2_lo = (o2 - o2_hi.astype(F32)).astype(BF16)
    ms = (jnp.dot(o2_hi, bd, preferred_element_type=F32)
          + jnp.dot(o2_lo, bd, preferred_element_type=F32)) * (1.0 / dh)
    hg = hg_ref[...]
    yh = (o * lax.rsqrt(ms + RMS_EPS) * ng_ref[...] * (hg * jax.nn.sigmoid(hg))).astype(BF16)

    mixed = jnp.concatenate([yc_ref[...], yh, yf], axis=1)
    mix = jnp.dot(mixed, wo_ref[...], preferred_element_type=F32)
    x1 = _layer_norm(alpha * x + (1.0 + g1) * mix, lg_ref[0:1, :], lb_ref[0:1, :])

    h2 = (x1 * (1.0 + sc2) + sh2).astype(BF16)
    acc_ref[...] = jnp.zeros_like(acc_ref)

    def ffn_chunk(ci, carry):
        gate = jnp.dot(h2, wg_ref[ci], preferred_element_type=F32)
        up = jnp.dot(h2, wu_ref[ci], preferred_element_type=F32)
        act = (gate * jax.nn.sigmoid(gate) * up).astype(BF16)
        acc_ref[...] += jnp.dot(act, wd_ref[ci], preferred_element_type=F32)
        return carry

    lax.fori_loop(0, wg_ref.shape[0], ffn_chunk, 0)
    out_ref[...] = _layer_norm(alpha * x1 + (1.0 + g2) * acc_ref[...], lg_ref[1:2, :], lb_ref[1:2, :])


def _outffn(x, yconv, oh, hg, yfox, mod, ln_in, norm_g, w_out_b, wg, wu, wd, ln_g, ln_b,
            *, layer, n_seq, sample, alpha):
    t, d = x.shape
    tm = TOKEN_TILE
    first_layer = ln_in is not None
    depth, rows, _ = mod.shape
    tiles_per_seq = (t // n_seq) // tm if not sample else 1
    tok = lambda a: pl.BlockSpec((tm, a.shape[1]), lambda i: (i, 0))
    const = lambda a: pl.BlockSpec(a.shape, lambda i: (0,) * a.ndim, pipeline_mode=pl.Buffered(1))
    small = lambda a: pl.BlockSpec(a.shape, lambda i: (0,) * a.ndim)

    inputs = [x, yconv, oh, hg, yfox]
    if sample:
        in_specs = [tok(x), tok(yconv), small(oh), tok(hg), small(yfox)]
        inputs += [mod, mod]
        in_specs += [pl.BlockSpec((None, n_seq, 2 * d), lambda i: (layer, 0, 1)),
                     pl.BlockSpec((None, n_seq, 2 * d), lambda i: (layer, 0, 2))]
    else:
        in_specs = [tok(a) for a in inputs]
        m4 = mod.reshape(depth, rows, 1, mod.shape[2])
        inputs += [m4, m4]
        row_of = lambda i: rows - SUBLANES + i // tiles_per_seq
        in_specs += [pl.BlockSpec((None, None, 1, 2 * d), lambda i: (layer, row_of(i), 0, 1)),
                     pl.BlockSpec((None, None, 1, 2 * d), lambda i: (layer, row_of(i), 0, 2))]
    if first_layer:
        inputs += [ln_in[0].reshape(1, d), ln_in[1].reshape(1, d)]
        in_specs += [pl.BlockSpec((1, d), lambda i: (0, 0))] * 2
    ng = norm_g.reshape(1, -1)
    inputs += [ng, w_out_b, wg, wu, wd, ln_g, ln_b]
    in_specs += [small(ng), const(w_out_b), const(wg), const(wu), const(wd), small(ln_g), small(ln_b)]
    kern = functools.partial(_outffn_kernel, sample=sample, first_layer=first_layer, d_model=d,
                             n_heads=HGRN_HEADS, alpha=alpha)
    return pl.pallas_call(
        kern,
        grid=(t // tm,),
        in_specs=in_specs,
        out_specs=pl.BlockSpec((tm, d), lambda i: (i, 0)),
        out_shape=jax.ShapeDtypeStruct((t, d), F32),
        scratch_shapes=[pltpu.VMEM((tm, d), F32)],
        compiler_params=_params(("arbitrary",)),
        name="outffn_sample" if sample else "outffn_prompt",
    )(*inputs)


def kernel(x_prompt, x_sample, c_prompt, c_sample, cache_k, cache_v, cache_logf, state_hgrn, state_conv,
           page_table, ln_in_g, ln_in_b, w_mod, b_mod, w_in, conv_w, hgrn_lb_logits, hgrn_norm_g,
           fox_f_bias, w_out, w_ffn_in, w_ffn_out, ln_g, ln_b):
    nbp, seq, d = x_prompt.shape
    nbs, n_steps, _ = x_sample.shape
    depth = w_mod.shape[0]
    alpha = (2 * depth) ** 0.25
    nh, dh = cache_k.shape[3:]
    dff = w_ffn_out.shape[1]
    nchunk = dff // FFN_CHUNK
    hgw = hgrn_lb_logits.shape[1]
    hdk = hgw // HGRN_HEADS

    c_all = jnp.concatenate([c_sample, c_prompt, jnp.zeros((SUBLANES - nbp, d), F32)], axis=0)
    mod = _modulation(c_all, w_mod, b_mod)

    ln_in = (ln_in_g, ln_in_b)
    xp = x_prompt.reshape(nbp * seq, d)
    xs = x_sample.transpose(1, 0, 2).reshape(n_steps * nbs, d)

    ck_t = cache_k.transpose(0, 1, 3, 4, 2)
    cv_t = cache_v.transpose(0, 1, 3, 4, 2)
    clf_t = cache_logf.transpose(0, 1, 3, 2)
    state_t = state_hgrn.transpose(0, 2, 3, 4, 1)

    outs_p = dict(k=[], v=[], lf=[], s=[], conv=[])
    outs_s = dict(k=[], v=[], lf=[], s=[], conv=[])
    for l in range(depth):
        w_in_p = jnp.pad(w_in[l], ((0, 0), (0, LANES - nh))).astype(BF16)
        fbias = jnp.pad(fox_f_bias[l], (0, LANES - nh)).reshape(1, LANES)
        w_out_b = w_out[l].astype(BF16)
        wg = w_ffn_in[l][:, :dff].reshape(d, nchunk, FFN_CHUNK).transpose(1, 0, 2).astype(BF16)
        wu = w_ffn_in[l][:, dff:].reshape(d, nchunk, FFN_CHUNK).transpose(1, 0, 2).astype(BF16)
        wd = w_ffn_out[l].reshape(nchunk, FFN_CHUNK, d).astype(BF16)
        lin = ln_in if l == 0 else None

        (yconv, ulast, qh, kh, lfh, vh, hg, qa, kat, va, fkt, fvt, lfft, cbase) = _inproj(
            xp, mod, lin, w_in_p, conv_w[l], hgrn_lb_logits, fbias, None, layer=l, n_seq=nbp, sample=False)
        oh, st_t = _hgrn_prompt(qh, kh, lfh, vh, n_seq=nbp)
        yfox = _fox_prompt(qa, kat, va, cbase, n_seq=nbp, n_heads=nh)
        xp = _outffn(xp, yconv, oh, hg, yfox, mod, lin, hgrn_norm_g[l], w_out_b, wg, wu, wd,
                     ln_g[l], ln_b[l], layer=l, n_seq=nbp, sample=False, alpha=alpha)
        st4 = st_t.reshape(nbp, HGRN_HEADS, hdk, HGRN_HEADS, hdk)
        outs_p["s"].append(jnp.stack([st4[:, hh, :, hh, :] for hh in range(HGRN_HEADS)], axis=1))
        outs_p["k"].append(fkt)
        outs_p["v"].append(fvt)
        outs_p["lf"].append(lfft)
        outs_p["conv"].append(ulast[:, SUBLANES - (CONV_K - 1):, :])

        prev = state_conv[l].transpose(1, 0, 2).reshape((CONV_K - 1) * nbs, -1)
        (yconv, u, qh, kh, lfh, vh, hg, fq, fk, fv, fkt, fvt, lfft) = _inproj(
            xs, mod, lin, w_in_p, conv_w[l], hgrn_lb_logits, fbias, prev, layer=l, n_seq=nbs, sample=True)
        oh, s_fin = _hgrn_sample(qh, kh, lfh, vh, state_t, layer=l)
        yfox = _fox_sample(page_table, ck_t, cv_t, clf_t, fq, fk, fv, lfft, layer=l)
        xs = _outffn(xs, yconv, oh, hg, yfox, mod, lin, hgrn_norm_g[l], w_out_b, wg, wu, wd,
                     ln_g[l], ln_b[l], layer=l, n_seq=nbs, sample=True, alpha=alpha)
        outs_s["s"].append(s_fin)
        outs_s["k"].append(fkt)
        outs_s["v"].append(fvt)
        outs_s["lf"].append(lfft)
        outs_s["conv"].append(u.reshape(n_steps, nbs, -1)[n_steps - (CONV_K - 1):].transpose(1, 0, 2))

    st = lambda xs_: jnp.stack(xs_)
    y_prompt = xp.reshape(nbp, seq, d)
    y_sample = xs.reshape(n_steps, nbs, d).transpose(1, 0, 2)
    kv_p = lambda a: st(a).reshape(depth, nbp, nh, dh, seq).transpose(0, 1, 4, 2, 3)
    kv_s = lambda a: st(a).reshape(depth, n_steps, nh, dh, nbs).transpose(0, 4, 1, 2, 3)
    return (y_prompt, y_sample,
            kv_p(outs_p["k"]), kv_p(outs_p["v"]), st(outs_p["lf"]).transpose(0, 1, 3, 2),
            st(outs_p["s"]), st(outs_p["conv"]),
            kv_s(outs_s["k"]), kv_s(outs_s["v"]), st(outs_s["lf"]).transpose(0, 3, 1, 2),
            st(outs_s["s"]).transpose(0, 4, 1, 2, 3), st(outs_s["conv"]))
```

```python
import functools

import jax
import jax.numpy as jnp
from jax import lax
from jax.experimental import pallas as pl
from jax.experimental.pallas import tpu as pltpu

F32 = jnp.float32
BF16 = jnp.bfloat16

HEAD_DIM = 64
HGRN_HEADS = 4
CONV_K = 3
LN_EPS = 1e-5
RMS_EPS = 1e-6
NEG_BIG = -1e30
LOG2E = 1.4426950408889634

LANES = 128
SUBLANES = 8
VMEM_LIMIT = 56 * 1024 * 1024
TOKEN_TILE = 512
HGRN_CHUNK = 128
MOD_TILE_N = 1536
FFN_CHUNK = 256
N_BIAS_PARTS = 3
FOX_SCORE_BUFFERS = 3

NT_DIMS = (((1,), (1,)), ((), ()))
TN_DIMS = (((0,), (0,)), ((), ()))


def _params(semantics):
    return pltpu.CompilerParams(dimension_semantics=semantics, vmem_limit_bytes=VMEM_LIMIT)


def _layer_norm(x, g, b):
    mu = jnp.mean(x, axis=-1, keepdims=True)
    xc = x - mu
    var = jnp.mean(xc * xc, axis=-1, keepdims=True)
    return xc * lax.rsqrt(var + LN_EPS) * g + b


def _log_sigmoid(x):
    return -(jnp.maximum(-x, 0.0) + jnp.log1p(jnp.exp(-jnp.abs(x))))


def _split3(x):
    hi = x.astype(BF16)
    r = x - hi.astype(F32)
    mid = r.astype(BF16)
    lo = (r - mid.astype(F32)).astype(BF16)
    return hi, mid, lo


def _dot_exact_lhs(a, parts):
    n = parts[0].shape[1]
    r = jnp.dot(a, jnp.concatenate(parts, axis=1), preferred_element_type=F32)
    return sum(r[:, j * n:(j + 1) * n] for j in range(len(parts)))


def _dot_exact_rhs(parts, a):
    m = parts[0].shape[0]
    r = jnp.dot(jnp.concatenate(parts, axis=0), a, preferred_element_type=F32)
    return sum(r[j * m:(j + 1) * m] for j in range(len(parts)))


def _split2(x):
    hi = x.astype(BF16)
    return hi, (x - hi.astype(F32)).astype(BF16)


def _iota(shape, axis):
    return lax.broadcasted_iota(jnp.int32, shape, axis)


def _log2(n):
    assert n > 0 and n & (n - 1) == 0, n
    return n.bit_length() - 1


def _div(x, n):
    return lax.shift_right_logical(x, _log2(n))


def _mod(x, n):
    return jnp.bitwise_and(x, n - 1)


def _mod_kernel(c_ref, w_ref, b_ref, o_ref):
    c = c_ref[...]
    a = (c * jax.nn.sigmoid(c)).astype(BF16)
    o_ref[...] = jnp.dot(a, w_ref[...].astype(BF16), preferred_element_type=F32) + b_ref[...]


def _modulation(c_all, w_mod, b_mod):
    depth, d, n = w_mod.shape
    rows = c_all.shape[0]
    return pl.pallas_call(
        _mod_kernel,
        grid=(depth, n // MOD_TILE_N),
        in_specs=[
            pl.BlockSpec((rows, d), lambda l, j: (0, 0)),
            pl.BlockSpec((None, d, MOD_TILE_N), lambda l, j: (l, 0, j)),
            pl.BlockSpec((None, 1, MOD_TILE_N), lambda l, j: (l, 0, j)),
        ],
        out_specs=pl.BlockSpec((None, rows, MOD_TILE_N), lambda l, j: (l, 0, j)),
        out_shape=jax.ShapeDtypeStruct((depth, rows, n), F32),
        compiler_params=_params(("arbitrary", "arbitrary")),
        name="modulation",
    )(c_all, w_mod, b_mod.reshape(depth, 1, n))


def _head_rows(x, hh, dh):
    per = LANES // dh
    slab = x[:, (hh // per) * LANES:(hh // per + 1) * LANES]
    shift = (hh % per) * dh
    return pltpu.roll(slab, LANES - shift, 1) if shift else slab


def _inproj_kernel(*refs, layer, sample, first_layer, tiles_per_seq, n_steps, d_model, conv_w, hgrn_w, fox_w):
    it = iter(refs)
    x_ref, mod_ref = next(it), next(it)
    if first_layer:
        lng_ref, lnb_ref = next(it), next(it)
    w_ref, cw_ref, lbl_ref, fb_ref = next(it), next(it), next(it), next(it)
    if sample:
        prev_ref = next(it)
        (yconv_ref, u_ref, qh_ref, kh_ref, lfh_ref, vh_ref, hg_ref,
         fq_ref, fk_ref, fv_ref, fkt_ref, fvt_ref, lfft_ref) = (next(it) for _ in range(13))
    else:
        (yconv_ref, ulast_ref, qh_ref, kh_ref, lfh_ref, vh_ref, hg_ref,
         qa_ref, kat_ref, va_ref, fkt_ref, fvt_ref, lfft_ref, cbase_ref,
         carry_ref, ccarry_ref) = (next(it) for _ in range(16))

    tm = x_ref.shape[0]
    dh = HEAD_DIM
    nh = fox_w // dh
    x = x_ref[...]
    if first_layer:
        x = _layer_norm(x, lng_ref[...], lnb_ref[...])
    m = mod_ref[...]
    if sample:
        m = jnp.concatenate([m] * (tm // m.shape[0]), axis=0)
    sh1, sc1 = m[:, :d_model], m[:, d_model:]
    h = (x * (1.0 + sc1) + sh1).astype(BF16)

    c0 = 0
    pc = jnp.dot(h, w_ref[:, c0:c0 + 3 * conv_w], preferred_element_type=F32)
    cb, cc, ch = pc[:, :conv_w], pc[:, conv_w:2 * conv_w], pc[:, 2 * conv_w:]
    u = cc * ch
    if sample:
        nb = tm // n_steps
        prev0, prev1 = prev_ref[:nb, :], prev_ref[nb:, :]
        u1 = jnp.concatenate([prev1, u[:tm - nb]], axis=0)
        u2 = jnp.concatenate([prev0, prev1, u[:tm - 2 * nb]], axis=0)
        u_ref[...] = u
    else:
        i = pl.program_id(0)

        @pl.when(i % tiles_per_seq == 0)
        def _():
            carry_ref[...] = jnp.zeros_like(carry_ref)
            ccarry_ref[...] = jnp.zeros_like(ccarry_ref)

        prev = carry_ref[...]
        row = _iota((tm, 1), 0)
        last1, last2 = prev[SUBLANES - 1:SUBLANES], prev[SUBLANES - 2:SUBLANES - 1]
        u1 = jnp.where(row == 0, last1, pltpu.roll(u, 1, 0))
        u2 = jnp.where(row == 0, last2, jnp.where(row == 1, last1, pltpu.roll(u, 2, 0)))
        carry_ref[...] = u[tm - SUBLANES:]
        ulast_ref[...] = u[tm - SUBLANES:]
    cw = cw_ref[...]
    yconv_ref[...] = (cb * (cw[0:1] * u2 + cw[1:2] * u1 + cw[2:3] * u)).astype(BF16)

    c0 += 3 * conv_w
    ph = jnp.dot(h, w_ref[:, c0:c0 + 4 * hgrn_w], preferred_element_type=F32)
    hq, hf = ph[:, :hgrn_w], ph[:, hgrn_w:2 * hgrn_w]
    hi, hg = ph[:, 2 * hgrn_w:3 * hgrn_w], ph[:, 3 * hgrn_w:]
    logits = lbl_ref[...]
    e = jnp.exp(logits - jnp.max(logits, axis=0, keepdims=True))
    p = e / jnp.sum(e, axis=0, keepdims=True)
    csum = p[0:1]
    for j in range(1, layer + 1):
        csum = csum + p[j:j + 1]
    lb = csum - p[layer:layer + 1]
    has_lb = lb > 0
    log_lb = jnp.where(has_lb, jnp.log(jnp.where(has_lb, lb, 1.0)), NEG_BIG)
    b2 = jnp.log1p(-lb) + _log_sigmoid(hf)
    lfh = jnp.maximum(log_lb, b2) + jnp.log1p(jnp.exp(-jnp.abs(log_lb - b2)))
    kh = (1.0 - lb) * jax.nn.sigmoid(-hf)
    qh = hq * jax.nn.sigmoid(hq)
    hg_ref[...] = hg
    if sample:
        for s in range(n_steps):
            rows = slice(s * nb, (s + 1) * nb)
            qh_ref[s] = qh[rows].T
            kh_ref[s] = kh[rows].T
            lfh_ref[s] = lfh[rows].T
            vh_ref[s] = hi[rows].T
    else:
        qh_ref[...] = qh
        kh_ref[...] = kh
        lfh_ref[...] = lfh
        vh_ref[...] = hi

    c0 += 4 * hgrn_w
    pf = jnp.dot(h, w_ref[:, c0:c0 + 3 * fox_w], preferred_element_type=F32)
    fq, fk, fv = pf[:, :fox_w], pf[:, fox_w:2 * fox_w], pf[:, 2 * fox_w:]
    c0 += 3 * fox_w
    pff = jnp.dot(h, w_ref[:, c0:c0 + LANES], preferred_element_type=F32)
    lff = _log_sigmoid(pff + fb_ref[...])
    if sample:
        fq_ref[...] = (fq * (dh ** -0.5)).reshape(fq_ref.shape)
        fk_ref[...] = fk.reshape(fk_ref.shape)
        fv_ref[...] = fv.reshape(fv_ref.shape)
        for s in range(n_steps):
            rows = slice(s * nb, (s + 1) * nb)
            fkt_ref[s] = fk[rows].T
            fvt_ref[s] = fv[rows].T
            lfft_ref[s] = lff[rows].T[:nh]
        return

    fkt = fk.T
    fkt_ref[...] = fkt
    fvt_ref[...] = fv.T
    lfft_ref[...] = lff.T[:nh]
    tri = (_iota((tm, tm), 0) >= _iota((tm, tm), 1)).astype(BF16)
    c_nat = _dot_exact_lhs(tri, _split3(lff))
    c_rel = c_nat * LOG2E
    c_rel_t = c_rel.T
    base = ccarry_ref[...]
    cbase_ref[...] = base * LOG2E
    ccarry_ref[...] = base + jnp.broadcast_to(c_nat[tm - 1:tm], ccarry_ref.shape)

    lane = _iota((tm, LANES), 1)
    srow = _iota((SUBLANES, tm), 0)
    npz = N_BIAS_PARTS
    for hh in range(nh):
        cq = [pp.astype(F32) for pp in _split3(jnp.broadcast_to(c_rel[:, hh:hh + 1], (tm, LANES)))]
        extra = jnp.where(lane < dh + 2 * npz, 1.0, 0.0)
        for j in range(npz):
            extra = jnp.where(lane == dh + j, cq[j], extra)
        q_slab = _head_rows(fq, hh, dh) * (LOG2E * dh ** -0.5)
        qa_ref[:, hh * LANES:(hh + 1) * LANES] = jnp.where(lane < dh, q_slab, extra).astype(BF16)
        v_slab = _head_rows(fv, hh, dh)
        va_ref[:, hh * LANES:(hh + 1) * LANES] = jnp.where(
            lane < dh, v_slab, jnp.where(lane == dh, 1.0, 0.0)).astype(BF16)
        ck = [pp.astype(F32) for pp in _split3(c_rel_t[hh:hh + 1, :])]
        grp = jnp.where(srow < npz, 1.0, 0.0)
        for j in range(npz):
            grp = jnp.where(srow == npz + j, -ck[j], grp)
        kat_ref[hh * LANES:(hh + 1) * LANES, :] = jnp.concatenate(
            [fkt[hh * dh:(hh + 1) * dh], grp, jnp.zeros((LANES - dh - SUBLANES, tm), F32)], axis=0).astype(BF16)


def _inproj(x, mod, ln_in, w_in_p, conv_w, lb_logits, fbias, prev, *, layer, n_seq, sample):
    t, d = x.shape
    tm = TOKEN_TILE
    n_tiles = t // tm
    seq = t // n_seq
    tiles_per_seq = seq // tm if not sample else 1
    n_steps = t // n_seq if sample else 0
    convw = conv_w.shape[1]
    hgw = lb_logits.shape[1]
    ncols = w_in_p.shape[1]
    foxw = (ncols - LANES - 3 * convw - 4 * hgw) // 3
    nh = foxw // HEAD_DIM
    first_layer = ln_in is not None
    depth = mod.shape[0]

    inputs, in_specs = [x], [pl.BlockSpec((tm, d), lambda i: (i, 0))]
    if sample:
        assert n_tiles == 1
        inputs.append(mod)
        in_specs.append(pl.BlockSpec((None, n_seq, 2 * d), lambda i: (layer, 0, 0)))
    else:
        rows = mod.shape[1]
        inputs.append(mod.reshape(depth, rows, 1, mod.shape[2]))
        in_specs.append(pl.BlockSpec((None, None, 1, 2 * d),
                                     lambda i: (layer, rows - SUBLANES + i // tiles_per_seq, 0, 0)))
    if first_layer:
        inputs += [ln_in[0].reshape(1, d), ln_in[1].reshape(1, d)]
        in_specs += [pl.BlockSpec((1, d), lambda i: (0, 0))] * 2
    inputs += [w_in_p, conv_w, lb_logits, fbias]
    in_specs += [
        pl.BlockSpec((d, ncols), lambda i: (0, 0), pipeline_mode=pl.Buffered(1)),
        pl.BlockSpec(conv_w.shape, lambda i: (0, 0)),
        pl.BlockSpec(lb_logits.shape, lambda i: (0, 0)),
        pl.BlockSpec((1, LANES), lambda i: (0, 0)),
    ]
    tok = lambda w, dt: (jax.ShapeDtypeStruct((t, w), dt), pl.BlockSpec((tm, w), lambda i: (i, 0)))
    whole = lambda shape, dt: (jax.ShapeDtypeStruct(shape, dt), pl.BlockSpec(shape, lambda i: (0,) * len(shape)))
    scratch = []
    if sample:
        inputs.append(prev)
        in_specs.append(pl.BlockSpec(prev.shape, lambda i: (0, 0)))
        outs = ([tok(convw, BF16), tok(convw, F32)] + [whole((n_steps, hgw, n_seq), F32)] * 4 + [tok(hgw, F32)]
                + [whole((n_steps, n_seq, foxw), F32)] * 3 + [whole((n_steps, foxw, n_seq), F32)] * 2
                + [whole((n_steps, nh, n_seq), F32)])
    else:
        per_seq_t = lambda rows, dt: (
            jax.ShapeDtypeStruct((n_seq, rows, seq), dt),
            pl.BlockSpec((None, rows, tm), lambda i: (i // tiles_per_seq, 0, i % tiles_per_seq)))
        outs = ([tok(convw, BF16),
                 (jax.ShapeDtypeStruct((n_seq, SUBLANES, convw), F32),
                  pl.BlockSpec((None, SUBLANES, convw), lambda i: (i // tiles_per_seq, 0, 0)))]
                + [tok(hgw, F32)] * 5
                + [tok(nh * LANES, BF16), per_seq_t(nh * LANES, BF16), tok(nh * LANES, BF16),
                   per_seq_t(foxw, F32), per_seq_t(foxw, F32), per_seq_t(nh, F32),
                   (jax.ShapeDtypeStruct((n_tiles, SUBLANES, LANES), F32),
                    pl.BlockSpec((None, SUBLANES, LANES), lambda i: (i, 0, 0)))])
        scratch = [pltpu.VMEM((SUBLANES, convw), F32), pltpu.VMEM((SUBLANES, LANES), F32)]
    kern = functools.partial(_inproj_kernel, layer=layer, sample=sample, first_layer=first_layer,
                             tiles_per_seq=tiles_per_seq, n_steps=n_steps, d_model=d, conv_w=convw,
                             hgrn_w=hgw, fox_w=foxw)
    return pl.pallas_call(
        kern,
        grid=(n_tiles,),
        in_specs=in_specs,
        out_specs=[o[1] for o in outs],
        out_shape=[o[0] for o in outs],
        scratch_shapes=scratch,
        compiler_params=_params(("arbitrary",)),
        name="inproj_sample" if sample else "inproj_prompt",
    )(*inputs)


def _hgrn_prompt_kernel(q_ref, k_ref, g_ref, v_ref, o_ref, st_out_ref, st_ref, *, chunk, n_heads):
    i = pl.program_id(1)
    tm, w = q_ref.shape
    dh = w // n_heads
    c = chunk

    @pl.when(i == 0)
    def _():
        st_ref[...] = jnp.zeros_like(st_ref)

    r_cc, c_cc = _iota((c, c), 0), _iota((c, c), 1)
    tri = (r_cc >= c_cc).astype(BF16)
    lane = _iota((1, w), 1)
    head_masks = [_div(lane, dh) == hh for hh in range(n_heads)]
    bd = _div(_iota((w, w), 0), dh) == _div(_iota((w, w), 1), dh)
    bd_bf = bd.astype(BF16)
    row = _iota((c, 1), 0)
    col_key = _mod(_iota((c, n_heads * c), 1), c)
    row_q = _iota((c, n_heads * c), 0)
    levels = []
    b = c // 2
    while b >= 1:
        levels.append(dict(
            half=b,
            upper=_mod(row, 2 * b) >= b,
            lower=_mod(row, 2 * b) < b,
            same=_div(row_q, 2 * b) == _div(col_key, 2 * b),
        ))
        b //= 2
    sub = _iota((1, SUBLANES, 1), 1)

    def boundary_rows(x, b):
        if 2 * b >= SUBLANES:
            return jnp.concatenate(
                [jnp.broadcast_to(x[s0 + b - 1:s0 + b], (2 * b, w)) for s0 in range(0, c, 2 * b)], axis=0)
        x3 = x.reshape(c // SUBLANES, SUBLANES, w)
        out = None
        for s0 in range(SUBLANES - 2 * b, -1, -2 * b):
            rep = jnp.broadcast_to(x3[:, s0 + b - 1:s0 + b, :], x3.shape)
            out = rep if out is None else jnp.where(sub < s0 + 2 * b, rep, out)
        return out.reshape(c, w)

    def chunk_body(ci, carry):
        r0 = pl.multiple_of(ci * c, c)
        q = q_ref[pl.ds(r0, c), :]
        k = k_ref[pl.ds(r0, c), :]
        g = g_ref[pl.ds(r0, c), :]
        v = v_ref[pl.ds(r0, c), :]
        vb = v.astype(BF16)
        gcum = _dot_exact_lhs(tri, _split3(g))
        a = jnp.zeros((c, n_heads * c), F32)
        for lv in levels:
            gref = boundary_rows(gcum, lv["half"])
            eq = jnp.exp(jnp.where(lv["upper"], gcum - gref, NEG_BIG))
            ek = jnp.exp(jnp.where(lv["lower"], gref - gcum, NEG_BIG))
            qt = (q * eq).astype(BF16)
            kt = (k * ek).astype(BF16)
            kbd = jnp.concatenate([jnp.where(hm, kt, jnp.zeros_like(kt)) for hm in head_masks], axis=0)
            xl = lax.dot_general(qt, kbd, NT_DIMS, preferred_element_type=F32)
            a = jnp.where(lv["same"], xl, a)
        diag = _dot_exact_rhs(_split2(q * k), bd_bf)
        o = diag * v
        vbd = jnp.concatenate([jnp.where(hm, vb, jnp.zeros_like(vb)) for hm in head_masks], axis=0)
        o = o + jnp.dot(a.astype(BF16), vbd, preferred_element_type=F32)
        st = st_ref[...]
        qg = (q * jnp.exp(gcum)).astype(BF16)
        o = o + lax.dot_general(qg, st.astype(BF16), NT_DIMS, preferred_element_type=F32)
        o_ref[pl.ds(r0, c), :] = o
        g_last = gcum[c - 1:c]
        kd = (k * jnp.exp(g_last - gcum)).astype(BF16)
        upd = lax.dot_general(vb, kd, TN_DIMS, preferred_element_type=F32)
        st_ref[...] = st * jnp.exp(g_last) + jnp.where(bd, upd, 0.0)
        return carry

    lax.fori_loop(0, tm // c, chunk_body, 0)

    @pl.when(i == pl.num_programs(1) - 1)
    def _():
        st_out_ref[...] = st_ref[...].T


def _hgrn_prompt(qh, kh, lfh, vh, *, n_seq):
    t, w = qh.shape
    tm = TOKEN_TILE
    tps = (t // n_seq) // tm
    spec = pl.BlockSpec((tm, w), lambda b, i: (b * tps + i, 0))
    kern = functools.partial(_hgrn_prompt_kernel, chunk=HGRN_CHUNK, n_heads=HGRN_HEADS)
    return pl.pallas_call(
        kern,
        grid=(n_seq, tps),
        in_specs=[spec] * 4,
        out_specs=[spec, pl.BlockSpec((None, w, w), lambda b, i: (b, 0, 0))],
        out_shape=[jax.ShapeDtypeStruct((t, w), F32), jax.ShapeDtypeStruct((n_seq, w, w), F32)],
        scratch_shapes=[pltpu.VMEM((w, w), F32)],
        compiler_params=_params(("arbitrary", "arbitrary")),
        name="hgrn_prompt",
    )(qh, kh, lfh, vh)


def _hgrn_sample_kernel(q_ref, k_ref, g_ref, v_ref, s0_ref, o_ref, s_out_ref):
    n_steps, dh, nb = q_ref.shape

    def body(d, o_acc):
        s = s0_ref[d]
        out = []
        for t in range(n_steps):
            row = pl.ds(d, 1)
            s = jnp.exp(g_ref[t, row, :]) * s + k_ref[t, row, :] * v_ref[t]
            out.append(o_acc[t] + q_ref[t, row, :] * s)
        s_out_ref[d] = s
        return tuple(out)

    o = lax.fori_loop(0, dh, body, tuple(jnp.zeros((v_ref.shape[1], nb), F32) for _ in range(n_steps)))
    for t in range(n_steps):
        o_ref[t] = o[t]


def _hgrn_sample(qh_t, kh_t, lfh_t, vh_t, state_t, *, layer):
    n_steps, w, nb = qh_t.shape
    _, n_heads, dk, dv, _ = state_t.shape
    gate = pl.BlockSpec((n_steps, dk, nb), lambda h: (0, h, 0))
    val = pl.BlockSpec((n_steps, dv, nb), lambda h: (0, h, 0))
    return pl.pallas_call(
        _hgrn_sample_kernel,
        grid=(n_heads,),
        in_specs=[gate, gate, gate, val,
                  pl.BlockSpec((None, None, dk, dv, nb), lambda h: (layer, h, 0, 0, 0))],
        out_specs=[val, pl.BlockSpec((None, dk, dv, nb), lambda h: (h, 0, 0, 0))],
        out_shape=[jax.ShapeDtypeStruct((n_steps, n_heads * dv, nb), F32),
                   jax.ShapeDtypeStruct((n_heads, dk, dv, nb), F32)],
        compiler_params=_params(("arbitrary",)),
        name="hgrn_sample",
    )(qh_t, kh_t, lfh_t, vh_t, state_t)


def _fox_prompt_kernel(qi_ref, kj_ref, last_ref, qa_ref, kat_ref, va_ref, cbq_ref, cbk_ref, o_ref,
                       m_scr, acc_scr, s_scr, mask_scr):
    pair = pl.program_id(1)
    qi, kj = qi_ref[pair], kj_ref[pair]
    tq, tk = qa_ref.shape[0], va_ref.shape[0]
    nh = m_scr.shape[0]
    dh = HEAD_DIM

    @pl.when((pl.program_id(0) == 0) & (pair == 0))
    def _():
        mask_scr[...] = jnp.where(_iota((tq, tk), 1) <= _iota((tq, tk), 0), 0.0, NEG_BIG)

    @pl.when(kj == 0)
    def _():
        m_scr[...] = jnp.full_like(m_scr, NEG_BIG)
        acc_scr[...] = jnp.zeros_like(acc_scr)

    delta = cbq_ref[0:1, :] - cbk_ref[0:1, :]

    nbuf = s_scr.shape[0]

    def tile(masked):
        def scores(hh):
            ls = slice(hh * LANES, (hh + 1) * LANES)
            s = jnp.dot(qa_ref[:, ls], kat_ref[ls, :], preferred_element_type=F32)
            if masked:
                s = s + mask_scr[...]
            s_scr[hh % nbuf] = s

        for hh in range(min(nbuf - 1, nh)):
            scores(hh)
        for hh in range(nh):
            if hh + nbuf - 1 < nh:
                scores(hh + nbuf - 1)
            ls = slice(hh * LANES, (hh + 1) * LANES)
            s = s_scr[hh % nbuf]
            d = delta[:, hh:hh + 1]
            m_prev = m_scr[hh]
            m_next = jnp.maximum(m_prev, jnp.max(s, axis=1, keepdims=True) + d)
            shift = m_next - d
            p = jnp.exp2(s - jnp.concatenate([shift] * (tk // LANES), axis=1))
            alpha = jnp.exp2(m_prev - m_next)
            pv = jnp.dot(p.astype(BF16), va_ref[:, ls], preferred_element_type=F32)
            acc_scr[hh] = acc_scr[hh] * alpha + pv
            m_scr[hh] = m_next

    pl.when(kj < qi)(lambda: tile(False))
    pl.when(kj == qi)(lambda: tile(True))

    @pl.when(last_ref[pair] == 1)
    def _():
        lane = _iota((tq, LANES), 1)
        per = LANES // dh
        for slab in range(nh // per):
            out = jnp.zeros((tq, LANES), F32)
            for j in range(per):
                acc = acc_scr[slab * per + j]
                val = acc / acc[:, dh:dh + 1]
                if j:
                    val = pltpu.roll(val, j * dh, 1)
                out = jnp.where(_div(lane, dh) == j, val, out)
            o_ref[:, slab * LANES:(slab + 1) * LANES] = out.astype(BF16)


def _fox_prompt(qa, kat, va, cbase, *, n_seq, n_heads):
    t, wa = qa.shape
    tq = tk = TOKEN_TILE
    nq = (t // n_seq) // tq
    pairs = [(i, j) for i in range(nq) for j in range(i + 1)]
    qi_tab = jnp.asarray([p[0] for p in pairs], jnp.int32)
    kj_tab = jnp.asarray([p[1] for p in pairs], jnp.int32)
    last_tab = jnp.asarray([int(p[0] == p[1]) for p in pairs], jnp.int32)
    grid_spec = pltpu.PrefetchScalarGridSpec(
        num_scalar_prefetch=3,
        grid=(n_seq, len(pairs)),
        in_specs=[
            pl.BlockSpec((tq, wa), lambda b, p, qi, kj, last: (b * nq + qi[p], 0)),
            pl.BlockSpec((None, wa, tk), lambda b, p, qi, kj, last: (b, 0, kj[p])),
            pl.BlockSpec((tk, wa), lambda b, p, qi, kj, last: (b * nq + kj[p], 0)),
            pl.BlockSpec((None, SUBLANES, LANES), lambda b, p, qi, kj, last: (b * nq + qi[p], 0, 0)),
            pl.BlockSpec((None, SUBLANES, LANES), lambda b, p, qi, kj, last: (b * nq + kj[p], 0, 0)),
        ],
        out_specs=pl.BlockSpec((tq, n_heads * HEAD_DIM), lambda b, p, qi, kj, last: (b * nq + qi[p], 0)),
        scratch_shapes=[pltpu.VMEM((n_heads, tq, LANES), F32), pltpu.VMEM((n_heads, tq, LANES), F32),
                        pltpu.VMEM((FOX_SCORE_BUFFERS, tq, tk), F32), pltpu.VMEM((tq, tk), F32)],
    )
    return pl.pallas_call(
        _fox_prompt_kernel,
        grid_spec=grid_spec,
        out_shape=jax.ShapeDtypeStruct((t, n_heads * HEAD_DIM), BF16),
        compiler_params=_params(("arbitrary", "arbitrary")),
        name="fox_prompt",
    )(qi_tab, kj_tab, last_tab, qa, kat, va, cbase, cbase)


def _fox_sample_kernel(pt_ref, *refs, n_pages, n_steps, n_heads):
    del pt_ref
    kp = refs[:n_pages]
    vp = refs[n_pages:2 * n_pages]
    lp = refs[2 * n_pages:3 * n_pages]
    q_ref, kn_ref, vn_ref, lfn_ref, o_ref, knew_scr, vnew_scr = refs[3 * n_pages:]
    _, dh, page = kp[0].shape
    w = n_heads * dh
    rows = n_steps * n_heads
    b = pl.program_id(0)
    r = pl.ds(_mod(b, SUBLANES), 1)

    @pl.when(b == 0)
    def _():
        knew_scr[...] = jnp.zeros_like(knew_scr)
        vnew_scr[...] = jnp.zeros_like(vnew_scr)

    hmask = _div(_iota((n_heads, w), 1), dh) == _iota((n_heads, w), 0)
    qbd = []
    for s in range(n_steps):
        knew_scr[s:s + 1, :] = kn_ref[s, r, :]
        vnew_scr[s:s + 1, :] = vn_ref[s, r, :]
        qbd.append(jnp.where(hmask, jnp.broadcast_to(q_ref[s, r, :], (n_heads, w)), 0.0))
    qbd = jnp.concatenate(qbd, axis=0).astype(BF16)

    lf_all = jnp.concatenate([lp[j][...] for j in range(n_pages)], axis=0)
    parts = _split3(lf_all)
    after = (_iota((page, page), 0) > _iota((page, page), 1)).astype(BF16)
    ones = jnp.ones((page, page), BF16)
    excl = sum(jnp.dot(pp, after, preferred_element_type=F32) for pp in parts)
    total = sum(jnp.dot(pp, ones, preferred_element_type=F32) for pp in parts)
    scores = [None] * n_pages
    later = jnp.zeros((n_heads, page), F32)
    for j in range(n_pages - 1, -1, -1):
        ps = slice(j * n_heads, (j + 1) * n_heads)
        bias = excl[ps] + later
        later = later + total[ps]
        kt = kp[j][...].reshape(w, page).astype(BF16)
        s = jnp.dot(qbd, kt, preferred_element_type=F32)
        scores[j] = s + jnp.concatenate([bias] * n_steps, axis=0)
    lane_h = _iota((n_heads, page), 1)
    seq_lane = _iota(lfn_ref.shape[1:], 1) == b
    bias_new = jnp.zeros((n_heads, page), F32)
    cum = jnp.zeros((n_heads, 1), F32)
    for s in range(n_steps):
        cum = cum + jnp.sum(jnp.where(seq_lane, lfn_ref[s], 0.0), axis=1, keepdims=True)
        bias_new = jnp.where(lane_h == s, -cum, bias_new)
    s_new = lax.dot_general(qbd, knew_scr[...].astype(BF16), NT_DIMS, preferred_element_type=F32)
    s_new = s_new + jnp.concatenate([bias_new] * n_steps, axis=0)
    key_pos = _iota((rows, page), 1)
    q_step = _div(_iota((rows, page), 0), n_heads)
    scores.append(jnp.where(key_pos <= q_step, s_new, NEG_BIG))

    m = scores[0].max(axis=1, keepdims=True)
    for s in scores[1:]:
        m = jnp.maximum(m, s.max(axis=1, keepdims=True))
    acc = jnp.zeros((rows, w), F32)
    den = jnp.zeros((rows, 1), F32)
    for j, s in enumerate(scores):
        p = jnp.exp(s - m)
        den = den + p.sum(axis=1, keepdims=True)
        pb = p.astype(BF16)
        if j < n_pages:
            vt = vp[j][...].reshape(w, page).astype(BF16)
            acc = acc + lax.dot_general(pb, vt, NT_DIMS, preferred_element_type=F32)
        else:
            acc = acc + jnp.dot(pb, vnew_scr[...].astype(BF16), preferred_element_type=F32)
    out = acc / den
    for s in range(n_steps):
        blk = jnp.where(hmask, out[s * n_heads:(s + 1) * n_heads], 0.0)
        o_ref[s, r, :] = blk.sum(axis=0, keepdims=True)


def _fox_sample(page_table, ck_t, cv_t, clf_t, fq, fk, fv, lff_t, *, layer):
    nb, n_pages = page_table.shape
    _, n_phys, n_heads, dh, page = ck_t.shape
    n_steps = fq.shape[0]
    w = n_heads * dh
    pt_flat = page_table.reshape(-1)

    def page_spec(shape, j):
        nd = len(shape)
        return pl.BlockSpec((None, None) + shape, lambda b, pt: (layer, pt[b * n_pages + j]) + (0,) * nd)

    in_specs = ([page_spec((n_heads, dh, page), j) for j in range(n_pages)] * 2
                + [page_spec((n_heads, page), j) for j in range(n_pages)])
    step_spec = pl.BlockSpec((n_steps, SUBLANES, w), lambda b, pt: (0, b // SUBLANES, 0))
    in_specs += [step_spec] * 3 + [pl.BlockSpec(lff_t.shape, lambda b, pt: (0, 0, 0))]
    kern = functools.partial(_fox_sample_kernel, n_pages=n_pages, n_steps=n_steps, n_heads=n_heads)
    grid_spec = pltpu.PrefetchScalarGridSpec(
        num_scalar_prefetch=1,
        grid=(nb,),
        in_specs=in_specs,
        out_specs=step_spec,
        scratch_shapes=[pltpu.VMEM((page, w), F32), pltpu.VMEM((page, w), F32)],
    )
    return pl.pallas_call(
        kern,
        grid_spec=grid_spec,
        out_shape=jax.ShapeDtypeStruct((n_steps, nb, w), F32),
        compiler_params=_params(("arbitrary",)),
        name="fox_sample",
    )(pt_flat, *([ck_t] * n_pages), *([cv_t] * n_pages), *([clf_t] * n_pages), fq, fk, fv, lff_t)


def _outffn_kernel(*refs, sample, first_layer, d_model, n_heads, alpha):
    it = iter(refs)
    x_ref, yc_ref, oh_ref, hg_ref, yf_ref, ma_ref, mb_ref = (next(it) for _ in range(7))
    if first_layer:
        lng_ref, lnb_ref = next(it), next(it)
    ng_ref, wo_ref, wg_ref, wu_ref, wd_ref, lg_ref, lb_ref, out_ref, acc_ref = (next(it) for _ in range(9))

    tm = x_ref.shape[0]
    x = x_ref[...]
    if first_layer:
        x = _layer_norm(x, lng_ref[...], lnb_ref[...])
    ma, mb = ma_ref[...], mb_ref[...]
    if sample:
        reps = tm // ma.shape[0]
        ma = jnp.concatenate([ma] * reps, axis=0)
        mb = jnp.concatenate([mb] * reps, axis=0)
        o = jnp.concatenate([oh_ref[s].T for s in range(oh_ref.shape[0])], axis=0)
        yf = yf_ref[...].reshape(tm, yf_ref.shape[2]).astype(BF16)
    else:
        o = oh_ref[...]
        yf = yf_ref[...]
    g1, sh2 = ma[:, :d_model], ma[:, d_model:]
    sc2, g2 = mb[:, :d_model], mb[:, d_model:]

    w = o.shape[1]
    dh = w // n_heads
    bd = (_div(_iota((w, w), 0), dh) == _div(_iota((w, w), 1), dh)).astype(BF16)
    ms = _dot_exact_rhs(_split2(o * o), bd) * (1.0 / dh)
    hg = hg_ref[...]
    yh = (o * lax.rsqrt(ms + RMS_EPS) * ng_ref[...] * (hg * jax.nn.sigmoid(hg))).astype(BF16)

    mixed = jnp.concatenate([yc_ref[...], yh, yf], axis=1)
    mix = jnp.dot(mixed, wo_ref[...], preferred_element_type=F32)
    x1 = _layer_norm(alpha * x + (1.0 + g1) * mix, lg_ref[0:1, :], lb_ref[0:1, :])

    h2 = (x1 * (1.0 + sc2) + sh2).astype(BF16)
    acc_ref[...] = jnp.zeros_like(acc_ref)

    def ffn_chunk(ci, carry):
        gate = jnp.dot(h2, wg_ref[ci], preferred_element_type=F32)
        up = jnp.dot(h2, wu_ref[ci], preferred_element_type=F32)
        act = (gate * jax.nn.sigmoid(gate) * up).astype(BF16)
        acc_ref[...] += jnp.dot(act, wd_ref[ci], preferred_element_type=F32)
        return carry

    lax.fori_loop(0, wg_ref.shape[0], ffn_chunk, 0)
    out_ref[...] = _layer_norm(alpha * x1 + (1.0 + g2) * acc_ref[...], lg_ref[1:2, :], lb_ref[1:2, :])


def _outffn(x, yconv, oh, hg, yfox, mod, ln_in, norm_g, w_out_b, wg, wu, wd, ln_g, ln_b,
            *, layer, n_seq, sample, alpha):
    t, d = x.shape
    tm = TOKEN_TILE
    first_layer = ln_in is not None
    depth, rows, _ = mod.shape
    tiles_per_seq = (t // n_seq) // tm if not sample else 1
    tok = lambda a: pl.BlockSpec((tm, a.shape[1]), lambda i: (i, 0))
    const = lambda a: pl.BlockSpec(a.shape, lambda i: (0,) * a.ndim, pipeline_mode=pl.Buffered(1))
    small = lambda a: pl.BlockSpec(a.shape, lambda i: (0,) * a.ndim)

    inputs = [x, yconv, oh, hg, yfox]
    if sample:
        in_specs = [tok(x), tok(yconv), small(oh), tok(hg), small(yfox)]
        inputs += [mod, mod]
        in_specs += [pl.BlockSpec((None, n_seq, 2 * d), lambda i: (layer, 0, 1)),
                     pl.BlockSpec((None, n_seq, 2 * d), lambda i: (layer, 0, 2))]
    else:
        in_specs = [tok(a) for a in inputs]
        m4 = mod.reshape(depth, rows, 1, mod.shape[2])
        inputs += [m4, m4]
        row_of = lambda i: rows - SUBLANES + i // tiles_per_seq
        in_specs += [pl.BlockSpec((None, None, 1, 2 * d), lambda i: (layer, row_of(i), 0, 1)),
                     pl.BlockSpec((None, None, 1, 2 * d), lambda i: (layer, row_of(i), 0, 2))]
    if first_layer:
        inputs += [ln_in[0].reshape(1, d), ln_in[1].reshape(1, d)]
        in_specs += [pl.BlockSpec((1, d), lambda i: (0, 0))] * 2
    ng = norm_g.reshape(1, -1)
    inputs += [ng, w_out_b, wg, wu, wd, ln_g, ln_b]
    in_specs += [small(ng), const(w_out_b), const(wg), const(wu), const(wd), small(ln_g), small(ln_b)]
    kern = functools.partial(_outffn_kernel, sample=sample, first_layer=first_layer, d_model=d,
                             n_heads=HGRN_HEADS, alpha=alpha)
    return pl.pallas_call(
        kern,
        grid=(t // tm,),
        in_specs=in_specs,
        out_specs=pl.BlockSpec((tm, d), lambda i: (i, 0)),
        out_shape=jax.ShapeDtypeStruct((t, d), F32),
        scratch_shapes=[pltpu.VMEM((tm, d), F32)],
        compiler_params=_params(("arbitrary",)),
        name="outffn_sample" if sample else "outffn_prompt",
    )(*inputs)


def kernel(x_prompt, x_sample, c_prompt, c_sample, cache_k, cache_v, cache_logf, state_hgrn, state_conv,
           page_table, ln_in_g, ln_in_b, w_mod, b_mod, w_in, conv_w, hgrn_lb_logits, hgrn_norm_g,
           fox_f_bias, w_out, w_ffn_in, w_ffn_out, ln_g, ln_b):
    nbp, seq, d = x_prompt.shape
    nbs, n_steps, _ = x_sample.shape
    depth = w_mod.shape[0]
    alpha = (2 * depth) ** 0.25
    nh, dh = cache_k.shape[3:]
    dff = w_ffn_out.shape[1]
    nchunk = dff // FFN_CHUNK
    hgw = hgrn_lb_logits.shape[1]
    hdk = hgw // HGRN_HEADS

    c_all = jnp.concatenate([c_sample, c_prompt, jnp.zeros((SUBLANES - nbp, d), F32)], axis=0)
    mod = _modulation(c_all, w_mod, b_mod)

    ln_in = (ln_in_g, ln_in_b)
    xp = x_prompt.reshape(nbp * seq, d)
    xs = x_sample.transpose(1, 0, 2).reshape(n_steps * nbs, d)

    ck_t = cache_k.transpose(0, 1, 3, 4, 2)
    cv_t = cache_v.transpose(0, 1, 3, 4, 2)
    clf_t = cache_logf.transpose(0, 1, 3, 2)
    state_t = state_hgrn.transpose(0, 2, 3, 4, 1)

    outs_p = dict(k=[], v=[], lf=[], s=[], conv=[])
    outs_s = dict(k=[], v=[], lf=[], s=[], conv=[])
    for l in range(depth):
        w_in_p = jnp.pad(w_in[l], ((0, 0), (0, LANES - nh))).astype(BF16)
        fbias = jnp.pad(fox_f_bias[l], (0, LANES - nh)).reshape(1, LANES)
        w_out_b = w_out[l].astype(BF16)
        wg = w_ffn_in[l][:, :dff].reshape(d, nchunk, FFN_CHUNK).transpose(1, 0, 2).astype(BF16)
        wu = w_ffn_in[l][:, dff:].reshape(d, nchunk, FFN_CHUNK).transpose(1, 0, 2).astype(BF16)
        wd = w_ffn_out[l].reshape(nchunk, FFN_CHUNK, d).astype(BF16)
        lin = ln_in if l == 0 else None

        (yconv, ulast, qh, kh, lfh, vh, hg, qa, kat, va, fkt, fvt, lfft, cbase) = _inproj(
            xp, mod, lin, w_in_p, conv_w[l], hgrn_lb_logits, fbias, None, layer=l, n_seq=nbp, sample=False)
        oh, st_t = _hgrn_prompt(qh, kh, lfh, vh, n_seq=nbp)
        yfox = _fox_prompt(qa, kat, va, cbase, n_seq=nbp, n_heads=nh)
        xp = _outffn(xp, yconv, oh, hg, yfox, mod, lin, hgrn_norm_g[l], w_out_b, wg, wu, wd,
                     ln_g[l], ln_b[l], layer=l, n_seq=nbp, sample=False, alpha=alpha)
        st4 = st_t.reshape(nbp, HGRN_HEADS, hdk, HGRN_HEADS, hdk)
        outs_p["s"].append(jnp.stack([st4[:, hh, :, hh, :] for hh in range(HGRN_HEADS)], axis=1))
        outs_p["k"].append(fkt)
        outs_p["v"].append(fvt)
        outs_p["lf"].append(lfft)
        outs_p["conv"].append(ulast[:, SUBLANES - (CONV_K - 1):, :])

        prev = state_conv[l].transpose(1, 0, 2).reshape((CONV_K - 1) * nbs, -1)
        (yconv, u, qh, kh, lfh, vh, hg, fq, fk, fv, fkt, fvt, lfft) = _inproj(
            xs, mod, lin, w_in_p, conv_w[l], hgrn_lb_logits, fbias, prev, layer=l, n_seq=nbs, sample=True)
        oh, s_fin = _hgrn_sample(qh, kh, lfh, vh, state_t, layer=l)
        yfox = _fox_sample(page_table, ck_t, cv_t, clf_t, fq, fk, fv, lfft, layer=l)
        xs = _outffn(xs, yconv, oh, hg, yfox, mod, lin, hgrn_norm_g[l], w_out_b, wg, wu, wd,
                     ln_g[l], ln_b[l], layer=l, n_seq=nbs, sample=True, alpha=alpha)
        outs_s["s"].append(s_fin)
        outs_s["k"].append(fkt)
        outs_s["v"].append(fvt)
        outs_s["lf"].append(lfft)
        outs_s["conv"].append(u.reshape(n_steps, nbs, -1)[n_steps - (CONV_K - 1):].transpose(1, 0, 2))

    st = lambda xs_: jnp.stack(xs_)
    y_prompt = xp.reshape(nbp, seq, d)
    y_sample = xs.reshape(n_steps, nbs, d).transpose(1, 0, 2)
    kv_p = lambda a: st(a).reshape(depth, nbp, nh, dh, seq).transpose(0, 1, 4, 2, 3)
    kv_s = lambda a: st(a).reshape(depth, n_steps, nh, dh, nbs).transpose(0, 4, 1, 2, 3)
    return (y_prompt, y_sample,
            kv_p(outs_p["k"]), kv_p(outs_p["v"]), st(outs_p["lf"]).transpose(0, 1, 3, 2),
            st(outs_p["s"]), st(outs_p["conv"]),
            kv_s(outs_s["k"]), kv_s(outs_s["v"]), st(outs_s["lf"]).transpose(0, 3, 1, 2),
            st(outs_s["s"]).transpose(0, 4, 1, 2, 3), st(outs_s["conv"]))
```

```python
import functools

import jax
import jax.numpy as jnp
from jax import lax
from jax.experimental import pallas as pl
from jax.experimental.pallas import tpu as pltpu

F32 = jnp.float32
BF16 = jnp.bfloat16

HEAD_DIM = 64
HGRN_HEADS = 4
CONV_K = 3
LN_EPS = 1e-5
RMS_EPS = 1e-6
NEG_BIG = -1e30
LOG2E = 1.4426950408889634

LANES = 128
SUBLANES = 8
VMEM_LIMIT = 56 * 1024 * 1024
TOKEN_TILE = 512
HGRN_CHUNK = 128
MOD_TILE_N = 1536
FFN_CHUNK = 256
N_BIAS_PARTS = 3
FOX_SCORE_BUFFERS = 3

NT_DIMS = (((1,), (1,)), ((), ()))
TN_DIMS = (((0,), (0,)), ((), ()))


def _params(semantics):
    return pltpu.CompilerParams(dimension_semantics=semantics, vmem_limit_bytes=VMEM_LIMIT)


def _layer_norm(x, g, b):
    mu = jnp.mean(x, axis=-1, keepdims=True)
    xc = x - mu
    var = jnp.mean(xc * xc, axis=-1, keepdims=True)
    return xc * lax.rsqrt(var + LN_EPS) * g + b


def _log_sigmoid(x):
    return -(jnp.maximum(-x, 0.0) + jnp.log1p(jnp.exp(-jnp.abs(x))))


def _split3(x):
    hi = x.astype(BF16)
    r = x - hi.astype(F32)
    mid = r.astype(BF16)
    lo = (r - mid.astype(F32)).astype(BF16)
    return hi, mid, lo


def _dot_exact_lhs(a, parts):
    n = parts[0].shape[1]
    r = jnp.dot(a, jnp.concatenate(parts, axis=1), preferred_element_type=F32)
    return sum(r[:, j * n:(j + 1) * n] for j in range(len(parts)))


def _dot_exact_rhs(parts, a):
    m = parts[0].shape[0]
    r = jnp.dot(jnp.concatenate(parts, axis=0), a, preferred_element_type=F32)
    return sum(r[j * m:(j + 1) * m] for j in range(len(parts)))


def _split2(x):
    hi = x.astype(BF16)
    return hi, (x - hi.astype(F32)).astype(BF16)


def _iota(shape, axis):
    return lax.broadcasted_iota(jnp.int32, shape, axis)


def _log2(n):
    assert n > 0 and n & (n - 1) == 0, n
    return n.bit_length() - 1


def _div(x, n):
    return lax.shift_right_logical(x, _log2(n))


def _mod(x, n):
    return jnp.bitwise_and(x, n - 1)


def _mod_kernel(c_ref, w_ref, b_ref, o_ref):
    c = c_ref[...]
    a = (c * jax.nn.sigmoid(c)).astype(BF16)
    o_ref[...] = jnp.dot(a, w_ref[...].astype(BF16), preferred_element_type=F32) + b_ref[...]


def _modulation(c_all, w_mod, b_mod):
    depth, d, n = w_mod.shape
    rows = c_all.shape[0]
    return pl.pallas_call(
        _mod_kernel,
        grid=(depth, n // MOD_TILE_N),
        in_specs=[
            pl.BlockSpec((rows, d), lambda l, j: (0, 0)),
            pl.BlockSpec((None, d, MOD_TILE_N), lambda l, j: (l, 0, j)),
            pl.BlockSpec((None, 1, MOD_TILE_N), lambda l, j: (l, 0, j)),
        ],
        out_specs=pl.BlockSpec((None, rows, MOD_TILE_N), lambda l, j: (l, 0, j)),
        out_shape=jax.ShapeDtypeStruct((depth, rows, n), F32),
        compiler_params=_params(("arbitrary", "arbitrary")),
        name="modulation",
    )(c_all, w_mod, b_mod.reshape(depth, 1, n))


def _head_rows(x, hh, dh):
    per = LANES // dh
    slab = x[:, (hh // per) * LANES:(hh // per + 1) * LANES]
    shift = (hh % per) * dh
    return pltpu.roll(slab, LANES - shift, 1) if shift else slab


def _inproj_kernel(*refs, layer, sample, first_layer, tiles_per_seq, n_steps, d_model, conv_w, hgrn_w, fox_w):
    it = iter(refs)
    x_ref, mod_ref = next(it), next(it)
    if first_layer:
        lng_ref, lnb_ref = next(it), next(it)
    w_ref, cw_ref, lbl_ref, fb_ref = next(it), next(it), next(it), next(it)
    if sample:
        prev_ref = next(it)
        (yconv_ref, u_ref, qh_ref, kh_ref, lfh_ref, vh_ref, hg_ref,
         fq_ref, fk_ref, fv_ref, fkt_ref, fvt_ref, lfft_ref) = (next(it) for _ in range(13))
    else:
        (yconv_ref, ulast_ref, qh_ref, kh_ref, lfh_ref, vh_ref, hg_ref,
         qat_ref, ka_ref, vat_ref, fkt_ref, fvt_ref, lfft_ref, cbase_ref,
         carry_ref, ccarry_ref) = (next(it) for _ in range(16))

    tm = x_ref.shape[0]
    dh = HEAD_DIM
    nh = fox_w // dh
    x = x_ref[...]
    if first_layer:
        x = _layer_norm(x, lng_ref[...], lnb_ref[...])
    m = mod_ref[...]
    if sample:
        m = jnp.concatenate([m] * (tm // m.shape[0]), axis=0)
    sh1, sc1 = m[:, :d_model], m[:, d_model:]
    h = (x * (1.0 + sc1) + sh1).astype(BF16)

    c0 = 0
    pc = jnp.dot(h, w_ref[:, c0:c0 + 3 * conv_w], preferred_element_type=F32)
    cb, cc, ch = pc[:, :conv_w], pc[:, conv_w:2 * conv_w], pc[:, 2 * conv_w:]
    u = cc * ch
    if sample:
        nb = tm // n_steps
        prev0, prev1 = prev_ref[:nb, :], prev_ref[nb:, :]
        u1 = jnp.concatenate([prev1, u[:tm - nb]], axis=0)
        u2 = jnp.concatenate([prev0, prev1, u[:tm - 2 * nb]], axis=0)
        u_ref[...] = u
    else:
        i = pl.program_id(0)

        @pl.when(i % tiles_per_seq == 0)
        def _():
            carry_ref[...] = jnp.zeros_like(carry_ref)
            ccarry_ref[...] = jnp.zeros_like(ccarry_ref)

        prev = carry_ref[...]
        row = _iota((tm, 1), 0)
        last1, last2 = prev[SUBLANES - 1:SUBLANES], prev[SUBLANES - 2:SUBLANES - 1]
        u1 = jnp.where(row == 0, last1, pltpu.roll(u, 1, 0))
        u2 = jnp.where(row == 0, last2, jnp.where(row == 1, last1, pltpu.roll(u, 2, 0)))
        carry_ref[...] = u[tm - SUBLANES:]
        ulast_ref[...] = u[tm - SUBLANES:]
    cw = cw_ref[...]
    yconv_ref[...] = (cb * (cw[0:1] * u2 + cw[1:2] * u1 + cw[2:3] * u)).astype(BF16)

    c0 += 3 * conv_w
    ph = jnp.dot(h, w_ref[:, c0:c0 + 4 * hgrn_w], preferred_element_type=F32)
    hq, hf = ph[:, :hgrn_w], ph[:, hgrn_w:2 * hgrn_w]
    hi, hg = ph[:, 2 * hgrn_w:3 * hgrn_w], ph[:, 3 * hgrn_w:]
    logits = lbl_ref[...]
    e = jnp.exp(logits - jnp.max(logits, axis=0, keepdims=True))
    p = e / jnp.sum(e, axis=0, keepdims=True)
    csum = p[0:1]
    for j in range(1, layer + 1):
        csum = csum + p[j:j + 1]
    lb = csum - p[layer:layer + 1]
    has_lb = lb > 0
    log_lb = jnp.where(has_lb, jnp.log(jnp.where(has_lb, lb, 1.0)), NEG_BIG)
    b2 = jnp.log1p(-lb) + _log_sigmoid(hf)
    lfh = jnp.maximum(log_lb, b2) + jnp.log1p(jnp.exp(-jnp.abs(log_lb - b2)))
    kh = (1.0 - lb) * jax.nn.sigmoid(-hf)
    qh = hq * jax.nn.sigmoid(hq)
    hg_ref[...] = hg
    if sample:
        for s in range(n_steps):
            rows = slice(s * nb, (s + 1) * nb)
            qh_ref[s] = qh[rows].T
            kh_ref[s] = kh[rows].T
            lfh_ref[s] = lfh[rows].T
            vh_ref[s] = hi[rows].T
    else:
        qh_ref[...] = qh
        kh_ref[...] = kh
        lfh_ref[...] = lfh
        vh_ref[...] = hi

    c0 += 4 * hgrn_w
    pf = jnp.dot(h, w_ref[:, c0:c0 + 3 * fox_w], preferred_element_type=F32)
    fq, fk, fv = pf[:, :fox_w], pf[:, fox_w:2 * fox_w], pf[:, 2 * fox_w:]
    c0 += 3 * fox_w
    pff = jnp.dot(h, w_ref[:, c0:c0 + LANES], preferred_element_type=F32)
    lff = _log_sigmoid(pff + fb_ref[...])
    if sample:
        fq_ref[...] = (fq * (dh ** -0.5)).reshape(fq_ref.shape)
        fk_ref[...] = fk.reshape(fk_ref.shape)
        fv_ref[...] = fv.reshape(fv_ref.shape)
        for s in range(n_steps):
            rows = slice(s * nb, (s + 1) * nb)
            fkt_ref[s] = fk[rows].T
            fvt_ref[s] = fv[rows].T
            lfft_ref[s] = lff[rows].T[:nh]
        return

    fvt = fv.T
    fqt = (fq * (LOG2E * dh ** -0.5)).T
    fkt_ref[...] = fk.T
    fvt_ref[...] = fvt
    lfft_ref[...] = lff.T[:nh]
    tri = (_iota((tm, tm), 0) >= _iota((tm, tm), 1)).astype(BF16)
    c_nat = _dot_exact_lhs(tri, _split3(lff))
    c_rel = c_nat * LOG2E
    c_rel_t = c_rel.T
    base = ccarry_ref[...]
    cbase_ref[...] = base * LOG2E
    ccarry_ref[...] = base + jnp.broadcast_to(c_nat[tm - 1:tm], ccarry_ref.shape)

    lane = _iota((tm, LANES), 1)
    srow = _iota((SUBLANES, tm), 0)
    npz = N_BIAS_PARTS
    pad_rows = jnp.zeros((LANES - dh - SUBLANES, tm), F32)
    for hh in range(nh):
        ck = [pp.astype(F32) for pp in _split3(jnp.broadcast_to(c_rel[:, hh:hh + 1], (tm, LANES)))]
        extra = jnp.where(lane < dh + npz, 1.0, 0.0)
        for j in range(npz):
            extra = jnp.where(lane == dh + npz + j, -ck[j], extra)
        ka_ref[:, hh * LANES:(hh + 1) * LANES] = jnp.where(lane < dh, _head_rows(fk, hh, dh), extra).astype(BF16)
        cq = [pp.astype(F32) for pp in _split3(c_rel_t[hh:hh + 1, :])]
        grp = jnp.where((srow >= npz) & (srow < 2 * npz), 1.0, 0.0)
        for j in range(npz):
            grp = jnp.where(srow == j, cq[j], grp)
        qat_ref[hh * LANES:(hh + 1) * LANES, :] = jnp.concatenate(
            [fqt[hh * dh:(hh + 1) * dh], grp, pad_rows], axis=0).astype(BF16)
        vat_ref[hh * LANES:(hh + 1) * LANES, :] = jnp.concatenate(
            [fvt[hh * dh:(hh + 1) * dh], jnp.where(srow == 0, 1.0, 0.0), pad_rows], axis=0).astype(BF16)


def _inproj(x, mod, ln_in, w_in_p, conv_w, lb_logits, fbias, prev, *, layer, n_seq, sample):
    t, d = x.shape
    tm = TOKEN_TILE
    n_tiles = t // tm
    seq = t // n_seq
    tiles_per_seq = seq // tm if not sample else 1
    n_steps = t // n_seq if sample else 0
    convw = conv_w.shape[1]
    hgw = lb_logits.shape[1]
    ncols = w_in_p.shape[1]
    foxw = (ncols - LANES - 3 * convw - 4 * hgw) // 3
    nh = foxw // HEAD_DIM
    first_layer = ln_in is not None
    depth = mod.shape[0]

    inputs, in_specs = [x], [pl.BlockSpec((tm, d), lambda i: (i, 0))]
    if sample:
        assert n_tiles == 1
        inputs.append(mod)
        in_specs.append(pl.BlockSpec((None, n_seq, 2 * d), lambda i: (layer, 0, 0)))
    else:
        rows = mod.shape[1]
        inputs.append(mod.reshape(depth, rows, 1, mod.shape[2]))
        in_specs.append(pl.BlockSpec((None, None, 1, 2 * d),
                                     lambda i: (layer, rows - SUBLANES + i // tiles_per_seq, 0, 0)))
    if first_layer:
        inputs += [ln_in[0].reshape(1, d), ln_in[1].reshape(1, d)]
        in_specs += [pl.BlockSpec((1, d), lambda i: (0, 0))] * 2
    inputs += [w_in_p, conv_w, lb_logits, fbias]
    in_specs += [
        pl.BlockSpec((d, ncols), lambda i: (0, 0), pipeline_mode=pl.Buffered(1)),
        pl.BlockSpec(conv_w.shape, lambda i: (0, 0)),
        pl.BlockSpec(lb_logits.shape, lambda i: (0, 0)),
        pl.BlockSpec((1, LANES), lambda i: (0, 0)),
    ]
    tok = lambda w, dt: (jax.ShapeDtypeStruct((t, w), dt), pl.BlockSpec((tm, w), lambda i: (i, 0)))
    whole = lambda shape, dt: (jax.ShapeDtypeStruct(shape, dt), pl.BlockSpec(shape, lambda i: (0,) * len(shape)))
    scratch = []
    if sample:
        inputs.append(prev)
        in_specs.append(pl.BlockSpec(prev.shape, lambda i: (0, 0)))
        outs = ([tok(convw, BF16), tok(convw, F32)] + [whole((n_steps, hgw, n_seq), F32)] * 4 + [tok(hgw, F32)]
                + [whole((n_steps, n_seq, foxw), F32)] * 3 + [whole((n_steps, foxw, n_seq), F32)] * 2
                + [whole((n_steps, nh, n_seq), F32)])
    else:
        per_seq_t = lambda rows, dt: (
            jax.ShapeDtypeStruct((n_seq, rows, seq), dt),
            pl.BlockSpec((None, rows, tm), lambda i: (i // tiles_per_seq, 0, i % tiles_per_seq)))
        outs = ([tok(convw, BF16),
                 (jax.ShapeDtypeStruct((n_seq, SUBLANES, convw), F32),
                  pl.BlockSpec((None, SUBLANES, convw), lambda i: (i // tiles_per_seq, 0, 0)))]
                + [tok(hgw, F32)] * 5
                + [per_seq_t(nh * LANES, BF16), tok(nh * LANES, BF16), per_seq_t(nh * LANES, BF16),
                   per_seq_t(foxw, F32), per_seq_t(foxw, F32), per_seq_t(nh, F32),
                   (jax.ShapeDtypeStruct((n_tiles, SUBLANES, LANES), F32),
                    pl.BlockSpec((None, SUBLANES, LANES), lambda i: (i, 0, 0)))])
        scratch = [pltpu.VMEM((SUBLANES, convw), F32), pltpu.VMEM((SUBLANES, LANES), F32)]
    kern = functools.partial(_inproj_kernel, layer=layer, sample=sample, first_layer=first_layer,
                             tiles_per_seq=tiles_per_seq, n_steps=n_steps, d_model=d, conv_w=convw,
                             hgrn_w=hgw, fox_w=foxw)
    return pl.pallas_call(
        kern,
        grid=(n_tiles,),
        in_specs=in_specs,
        out_specs=[o[1] for o in outs],
        out_shape=[o[0] for o in outs],
        scratch_shapes=scratch,
        compiler_params=_params(("arbitrary",)),
        name="inproj_sample" if sample else "inproj_prompt",
    )(*inputs)


def _hgrn_prompt_kernel(q_ref, k_ref, g_ref, v_ref, o_ref, st_out_ref, st_ref, *, chunk, n_heads):
    i = pl.program_id(1)
    tm, w = q_ref.shape
    dh = w // n_heads
    c = chunk

    @pl.when(i == 0)
    def _():
        st_ref[...] = jnp.zeros_like(st_ref)

    r_cc, c_cc = _iota((c, c), 0), _iota((c, c), 1)
    tri = (r_cc >= c_cc).astype(BF16)
    lane = _iota((1, w), 1)
    head_masks = [_div(lane, dh) == hh for hh in range(n_heads)]
    bd = _div(_iota((w, w), 0), dh) == _div(_iota((w, w), 1), dh)
    bd_bf = bd.astype(BF16)
    row = _iota((c, 1), 0)
    col_key = _mod(_iota((c, n_heads * c), 1), c)
    row_q = _iota((c, n_heads * c), 0)
    levels = []
    b = c // 2
    while b >= 1:
        levels.append(dict(
            half=b,
            upper=_mod(row, 2 * b) >= b,
            lower=_mod(row, 2 * b) < b,
            same=_div(row_q, 2 * b) == _div(col_key, 2 * b),
        ))
        b //= 2
    sub = _iota((1, SUBLANES, 1), 1)

    def boundary_rows(x, b):
        if 2 * b >= SUBLANES:
            return jnp.concatenate(
                [jnp.broadcast_to(x[s0 + b - 1:s0 + b], (2 * b, w)) for s0 in range(0, c, 2 * b)], axis=0)
        x3 = x.reshape(c // SUBLANES, SUBLANES, w)
        out = None
        for s0 in range(SUBLANES - 2 * b, -1, -2 * b):
            rep = jnp.broadcast_to(x3[:, s0 + b - 1:s0 + b, :], x3.shape)
            out = rep if out is None else jnp.where(sub < s0 + 2 * b, rep, out)
        return out.reshape(c, w)

    def chunk_body(ci, carry):
        r0 = pl.multiple_of(ci * c, c)
        q = q_ref[pl.ds(r0, c), :]
        k = k_ref[pl.ds(r0, c), :]
        g = g_ref[pl.ds(r0, c), :]
        v = v_ref[pl.ds(r0, c), :]
        vb = v.astype(BF16)
        gcum = _dot_exact_lhs(tri, _split3(g))
        a = jnp.zeros((c, n_heads * c), F32)
        for lv in levels:
            gref = boundary_rows(gcum, lv["half"])
            eq = jnp.exp(jnp.where(lv["upper"], gcum - gref, NEG_BIG))
            ek = jnp.exp(jnp.where(lv["lower"], gref - gcum, NEG_BIG))
            qt = (q * eq).astype(BF16)
            kt = (k * ek).astype(BF16)
            kbd = jnp.concatenate([jnp.where(hm, kt, jnp.zeros_like(kt)) for hm in head_masks], axis=0)
            xl = lax.dot_general(qt, kbd, NT_DIMS, preferred_element_type=F32)
            a = jnp.where(lv["same"], xl, a)
        diag = _dot_exact_rhs(_split2(q * k), bd_bf)
        o = diag * v
        vbd = jnp.concatenate([jnp.where(hm, vb, jnp.zeros_like(vb)) for hm in head_masks], axis=0)
        o = o + jnp.dot(a.astype(BF16), vbd, preferred_element_type=F32)
        st = st_ref[...]
        qg = (q * jnp.exp(gcum)).astype(BF16)
        o = o + lax.dot_general(qg, st.astype(BF16), NT_DIMS, preferred_element_type=F32)
        o_ref[pl.ds(r0, c), :] = o
        g_last = gcum[c - 1:c]
        kd = (k * jnp.exp(g_last - gcum)).astype(BF16)
        upd = lax.dot_general(vb, kd, TN_DIMS, preferred_element_type=F32)
        st_ref[...] = st * jnp.exp(g_last) + jnp.where(bd, upd, 0.0)
        return carry

    lax.fori_loop(0, tm // c, chunk_body, 0)

    @pl.when(i == pl.num_programs(1) - 1)
    def _():
        st_out_ref[...] = st_ref[...].T


def _hgrn_prompt(qh, kh, lfh, vh, *, n_seq):
    t, w = qh.shape
    tm = TOKEN_TILE
    tps = (t // n_seq) // tm
    spec = pl.BlockSpec((tm, w), lambda b, i: (b * tps + i, 0))
    kern = functools.partial(_hgrn_prompt_kernel, chunk=HGRN_CHUNK, n_heads=HGRN_HEADS)
    return pl.pallas_call(
        kern,
        grid=(n_seq, tps),
        in_specs=[spec] * 4,
        out_specs=[spec, pl.BlockSpec((None, w, w), lambda b, i: (b, 0, 0))],
        out_shape=[jax.ShapeDtypeStruct((t, w), F32), jax.ShapeDtypeStruct((n_seq, w, w), F32)],
        scratch_shapes=[pltpu.VMEM((w, w), F32)],
        compiler_params=_params(("arbitrary", "arbitrary")),
        name="hgrn_prompt",
    )(qh, kh, lfh, vh)


def _hgrn_sample_kernel(q_ref, k_ref, g_ref, v_ref, s0_ref, o_ref, s_out_ref):
    n_steps, dh, nb = q_ref.shape

    def body(d, o_acc):
        s = s0_ref[d]
        out = []
        for t in range(n_steps):
            row = pl.ds(d, 1)
            s = jnp.exp(g_ref[t, row, :]) * s + k_ref[t, row, :] * v_ref[t]
            out.append(o_acc[t] + q_ref[t, row, :] * s)
        s_out_ref[d] = s
        return tuple(out)

    o = lax.fori_loop(0, dh, body, tuple(jnp.zeros((v_ref.shape[1], nb), F32) for _ in range(n_steps)))
    for t in range(n_steps):
        o_ref[t] = o[t]


def _hgrn_sample(qh_t, kh_t, lfh_t, vh_t, state_t, *, layer):
    n_steps, w, nb = qh_t.shape
    _, n_heads, dk, dv, _ = state_t.shape
    gate = pl.BlockSpec((n_steps, dk, nb), lambda h: (0, h, 0))
    val = pl.BlockSpec((n_steps, dv, nb), lambda h: (0, h, 0))
    return pl.pallas_call(
        _hgrn_sample_kernel,
        grid=(n_heads,),
        in_specs=[gate, gate, gate, val,
                  pl.BlockSpec((None, None, dk, dv, nb), lambda h: (layer, h, 0, 0, 0))],
        out_specs=[val, pl.BlockSpec((None, dk, dv, nb), lambda h: (h, 0, 0, 0))],
        out_shape=[jax.ShapeDtypeStruct((n_steps, n_heads * dv, nb), F32),
                   jax.ShapeDtypeStruct((n_heads, dk, dv, nb), F32)],
        compiler_params=_params(("arbitrary",)),
        name="hgrn_sample",
    )(qh_t, kh_t, lfh_t, vh_t, state_t)


def _fox_prompt_kernel(qi_ref, kj_ref, last_ref, ka_ref, qat_ref, vat_ref, cbq_ref, cbk_ref, o_ref,
                       m_scr, acc_scr, s_scr, mask_scr):
    pair = pl.program_id(1)
    qi, kj = qi_ref[pair], kj_ref[pair]
    tk, tq = ka_ref.shape[0], qat_ref.shape[1]
    nh = m_scr.shape[0]
    dh = HEAD_DIM

    @pl.when((pl.program_id(0) == 0) & (pair == 0))
    def _():
        mask_scr[...] = jnp.where(_iota((tk, tq), 0) <= _iota((tk, tq), 1), 0.0, NEG_BIG)

    @pl.when(kj == 0)
    def _():
        m_scr[...] = jnp.full_like(m_scr, NEG_BIG)
        acc_scr[...] = jnp.zeros_like(acc_scr)

    delta = cbq_ref[0:1, :] - cbk_ref[0:1, :]

    nbuf = s_scr.shape[0]

    def tile(masked):
        def scores(hh):
            ls = slice(hh * LANES, (hh + 1) * LANES)
            s = jnp.dot(ka_ref[:, ls], qat_ref[ls, :], preferred_element_type=F32)
            if masked:
                s = s + mask_scr[...]
            s_scr[hh % nbuf] = s

        for hh in range(min(nbuf - 1, nh)):
            scores(hh)
        for hh in range(nh):
            if hh + nbuf - 1 < nh:
                scores(hh + nbuf - 1)
            ls = slice(hh * LANES, (hh + 1) * LANES)
            s = s_scr[hh % nbuf]
            d = delta[:, hh:hh + 1]
            m_prev = m_scr[hh]
            m_next = jnp.maximum(m_prev, jnp.max(s, axis=0, keepdims=True) + d)
            p = jnp.exp2(s - (m_next[0:1] - d))
            alpha = jnp.exp2(m_prev - m_next)
            pv = jnp.dot(vat_ref[ls, :], p.astype(BF16), preferred_element_type=F32)
            acc_scr[hh] = acc_scr[hh] * alpha[0:1] + pv
            m_scr[hh] = m_next

    pl.when(kj < qi)(lambda: tile(False))
    pl.when(kj == qi)(lambda: tile(True))

    @pl.when(last_ref[pair] == 1)
    def _():
        per = LANES // dh
        for slab in range(nh // per):
            vals = []
            for j in range(per):
                acc = acc_scr[slab * per + j]
                vals.append(acc[:dh] / acc[dh:dh + 1])
            o_ref[:, slab * LANES:(slab + 1) * LANES] = jnp.concatenate(vals, axis=0).T.astype(BF16)


def _fox_prompt(ka, qat, vat, cbase, *, n_seq, n_heads):
    t, wa = ka.shape
    tq = tk = TOKEN_TILE
    nq = (t // n_seq) // tq
    pairs = [(i, j) for i in range(nq) for j in range(i + 1)]
    qi_tab = jnp.asarray([p[0] for p in pairs], jnp.int32)
    kj_tab = jnp.asarray([p[1] for p in pairs], jnp.int32)
    last_tab = jnp.asarray([int(p[0] == p[1]) for p in pairs], jnp.int32)
    grid_spec = pltpu.PrefetchScalarGridSpec(
        num_scalar_prefetch=3,
        grid=(n_seq, len(pairs)),
        in_specs=[
            pl.BlockSpec((tk, wa), lambda b, p, qi, kj, last: (b * nq + kj[p], 0)),
            pl.BlockSpec((None, wa, tq), lambda b, p, qi, kj, last: (b, 0, qi[p])),
            pl.BlockSpec((None, wa, tk), lambda b, p, qi, kj, last: (b, 0, kj[p])),
            pl.BlockSpec((None, SUBLANES, LANES), lambda b, p, qi, kj, last: (b * nq + qi[p], 0, 0)),
            pl.BlockSpec((None, SUBLANES, LANES), lambda b, p, qi, kj, last: (b * nq + kj[p], 0, 0)),
        ],
        out_specs=pl.BlockSpec((tq, n_heads * HEAD_DIM), lambda b, p, qi, kj, last: (b * nq + qi[p], 0)),
        scratch_shapes=[pltpu.VMEM((n_heads, SUBLANES, tq), F32), pltpu.VMEM((n_heads, LANES, tq), F32),
                        pltpu.VMEM((FOX_SCORE_BUFFERS, tk, tq), F32), pltpu.VMEM((tk, tq), F32)],
    )
    return pl.pallas_call(
        _fox_prompt_kernel,
        grid_spec=grid_spec,
        out_shape=jax.ShapeDtypeStruct((t, n_heads * HEAD_DIM), BF16),
        compiler_params=_params(("arbitrary", "arbitrary")),
        name="fox_prompt",
    )(qi_tab, kj_tab, last_tab, ka, qat, vat, cbase, cbase)


def _fox_sample_kernel(pt_ref, *refs, n_pages, n_steps, n_heads):
    del pt_ref
    kp = refs[:n_pages]
    vp = refs[n_pages:2 * n_pages]
    lp = refs[2 * n_pages:3 * n_pages]
    q_ref, kn_ref, vn_ref, lfn_ref, o_ref, knew_scr, vnew_scr = refs[3 * n_pages:]
    _, dh, page = kp[0].shape
    w = n_heads * dh
    rows = n_steps * n_heads
    b = pl.program_id(0)
    r = pl.ds(_mod(b, SUBLANES), 1)

    @pl.when(b == 0)
    def _():
        knew_scr[...] = jnp.zeros_like(knew_scr)
        vnew_scr[...] = jnp.zeros_like(vnew_scr)

    hmask = _div(_iota((n_heads, w), 1), dh) == _iota((n_heads, w), 0)
    qbd = []
    for s in range(n_steps):
        knew_scr[s:s + 1, :] = kn_ref[s, r, :]
        vnew_scr[s:s + 1, :] = vn_ref[s, r, :]
        qbd.append(jnp.where(hmask, jnp.broadcast_to(q_ref[s, r, :], (n_heads, w)), 0.0))
    qbd = jnp.concatenate(qbd, axis=0).astype(BF16)

    lf_all = jnp.concatenate([lp[j][...] for j in range(n_pages)], axis=0)
    parts = _split3(lf_all)
    after = (_iota((page, page), 0) > _iota((page, page), 1)).astype(BF16)
    ones = jnp.ones((page, page), BF16)
    excl = sum(jnp.dot(pp, after, preferred_element_type=F32) for pp in parts)
    total = sum(jnp.dot(pp, ones, preferred_element_type=F32) for pp in parts)
    scores = [None] * n_pages
    later = jnp.zeros((n_heads, page), F32)
    for j in range(n_pages - 1, -1, -1):
        ps = slice(j * n_heads, (j + 1) * n_heads)
        bias = excl[ps] + later
        later = later + total[ps]
        kt = kp[j][...].reshape(w, page).astype(BF16)
        s = jnp.dot(qbd, kt, preferred_element_type=F32)
        scores[j] = s + jnp.concatenate([bias] * n_steps, axis=0)
    lane_h = _iota((n_heads, page), 1)
    seq_lane = _iota(lfn_ref.shape[1:], 1) == b
    bias_new = jnp.zeros((n_heads, page), F32)
    cum = jnp.zeros((n_heads, 1), F32)
    for s in range(n_steps):
        cum = cum + jnp.sum(jnp.where(seq_lane, lfn_ref[s], 0.0), axis=1, keepdims=True)
        bias_new = jnp.where(lane_h == s, -cum, bias_new)
    s_new = lax.dot_general(qbd, knew_scr[...].astype(BF16), NT_DIMS, preferred_element_type=F32)
    s_new = s_new + jnp.concatenate([bias_new] * n_steps, axis=0)
    key_pos = _iota((rows, page), 1)
    q_step = _div(_iota((rows, page), 0), n_heads)
    scores.append(jnp.where(key_pos <= q_step, s_new, NEG_BIG))

    m = scores[0].max(axis=1, keepdims=True)
    for s in scores[1:]:
        m = jnp.maximum(m, s.max(axis=1, keepdims=True))
    acc = jnp.zeros((rows, w), F32)
    den = jnp.zeros((rows, 1), F32)
    for j, s in enumerate(scores):
        p = jnp.exp(s - m)
        den = den + p.sum(axis=1, keepdims=True)
        pb = p.astype(BF16)
        if j < n_pages:
            vt = vp[j][...].reshape(w, page).astype(BF16)
            acc = acc + lax.dot_general(pb, vt, NT_DIMS, preferred_element_type=F32)
        else:
            acc = acc + jnp.dot(pb, vnew_scr[...].astype(BF16), preferred_element_type=F32)
    out = acc / den
    for s in range(n_steps):
        blk = jnp.where(hmask, out[s * n_heads:(s + 1) * n_heads], 0.0)
        o_ref[s, r, :] = blk.sum(axis=0, keepdims=True)


def _fox_sample(page_table, ck_t, cv_t, clf_t, fq, fk, fv, lff_t, *, layer):
    nb, n_pages = page_table.shape
    _, n_phys, n_heads, dh, page = ck_t.shape
    n_steps = fq.shape[0]
    w = n_heads * dh
    pt_flat = page_table.reshape(-1)

    def page_spec(shape, j):
        nd = len(shape)
        return pl.BlockSpec((None, None) + shape, lambda b, pt: (layer, pt[b * n_pages + j]) + (0,) * nd)

    in_specs = ([page_spec((n_heads, dh, page), j) for j in range(n_pages)] * 2
                + [page_spec((n_heads, page), j) for j in range(n_pages)])
    step_spec = pl.BlockSpec((n_steps, SUBLANES, w), lambda b, pt: (0, b // SUBLANES, 0))
    in_specs += [step_spec] * 3 + [pl.BlockSpec(lff_t.shape, lambda b, pt: (0, 0, 0))]
    kern = functools.partial(_fox_sample_kernel, n_pages=n_pages, n_steps=n_steps, n_heads=n_heads)
    grid_spec = pltpu.PrefetchScalarGridSpec(
        num_scalar_prefetch=1,
        grid=(nb,),
        in_specs=in_specs,
        out_specs=step_spec,
        scratch_shapes=[pltpu.VMEM((page, w), F32), pltpu.VMEM((page, w), F32)],
    )
    return pl.pallas_call(
        kern,
        grid_spec=grid_spec,
        out_shape=jax.ShapeDtypeStruct((n_steps, nb, w), F32),
        compiler_params=_params(("arbitrary",)),
        name="fox_sample",
    )(pt_flat, *([ck_t] * n_pages), *([cv_t] * n_pages), *([clf_t] * n_pages), fq, fk, fv, lff_t)


def _outffn_kernel(*refs, sample, first_layer, d_model, n_heads, alpha):
    it = iter(refs)
    x_ref, yc_ref, oh_ref, hg_ref, yf_ref, ma_ref, mb_ref = (next(it) for _ in range(7))
    if first_layer:
        lng_ref, lnb_ref = next(it), next(it)
    ng_ref, wo_ref, wg_ref, wu_ref, wd_ref, lg_ref, lb_ref, out_ref, acc_ref = (next(it) for _ in range(9))

    tm = x_ref.shape[0]
    x = x_ref[...]
    if first_layer:
        x = _layer_norm(x, lng_ref[...], lnb_ref[...])
    ma, mb = ma_ref[...], mb_ref[...]
    if sample:
        reps = tm // ma.shape[0]
        ma = jnp.concatenate([ma] * reps, axis=0)
        mb = jnp.concatenate([mb] * reps, axis=0)
        o = jnp.concatenate([oh_ref[s].T for s in range(oh_ref.shape[0])], axis=0)
        yf = yf_ref[...].reshape(tm, yf_ref.shape[2]).astype(BF16)
    else:
        o = oh_ref[...]
        yf = yf_ref[...]
    g1, sh2 = ma[:, :d_model], ma[:, d_model:]
    sc2, g2 = mb[:, :d_model], mb[:, d_model:]

    w = o.shape[1]
    dh = w // n_heads
    bd = (_div(_iota((w, w), 0), dh) == _div(_iota((w, w), 1), dh)).astype(BF16)
    ms = _dot_exact_rhs(_split2(o * o), bd) * (1.0 / dh)
    hg = hg_ref[...]
    yh = (o * lax.rsqrt(ms + RMS_EPS) * ng_ref[...] * (hg * jax.nn.sigmoid(hg))).astype(BF16)

    mixed = jnp.concatenate([yc_ref[...], yh, yf], axis=1)
    mix = jnp.dot(mixed, wo_ref[...], preferred_element_type=F32)
    x1 = _layer_norm(alpha * x + (1.0 + g1) * mix, lg_ref[0:1, :], lb_ref[0:1, :])

    h2 = (x1 * (1.0 + sc2) + sh2).astype(BF16)
    acc_ref[...] = jnp.zeros_like(acc_ref)

    def ffn_chunk(ci, carry):
        gate = jnp.dot(h2, wg_ref[ci], preferred_element_type=F32)
        up = jnp.dot(h2, wu_ref[ci], preferred_element_type=F32)
        act = (gate * jax.nn.sigmoid(gate) * up).astype(BF16)
        acc_ref[...] += jnp.dot(act, wd_ref[ci], preferred_element_type=F32)
        return carry

    lax.fori_loop(0, wg_ref.shape[0], ffn_chunk, 0)
    out_ref[...] = _layer_norm(alpha * x1 + (1.0 + g2) * acc_ref[...], lg_ref[1:2, :], lb_ref[1:2, :])


def _outffn(x, yconv, oh, hg, yfox, mod, ln_in, norm_g, w_out_b, wg, wu, wd, ln_g, ln_b,
            *, layer, n_seq, sample, alpha):
    t, d = x.shape
    tm = TOKEN_TILE
    first_layer = ln_in is not None
    depth, rows, _ = mod.shape
    tiles_per_seq = (t // n_seq) // tm if not sample else 1
    tok = lambda a: pl.BlockSpec((tm, a.shape[1]), lambda i: (i, 0))
    const = lambda a: pl.BlockSpec(a.shape, lambda i: (0,) * a.ndim, pipeline_mode=pl.Buffered(1))
    small = lambda a: pl.BlockSpec(a.shape, lambda i: (0,) * a.ndim)

    inputs = [x, yconv, oh, hg, yfox]
    if sample:
        in_specs = [tok(x), tok(yconv), small(oh), tok(hg), small(yfox)]
        inputs += [mod, mod]
        in_specs += [pl.BlockSpec((None, n_seq, 2 * d), lambda i: (layer, 0, 1)),
                     pl.BlockSpec((None, n_seq, 2 * d), lambda i: (layer, 0, 2))]
    else:
        in_specs = [tok(a) for a in inputs]
        m4 = mod.reshape(depth, rows, 1, mod.shape[2])
        inputs += [m4, m4]
        row_of = lambda i: rows - SUBLANES + i // tiles_per_seq
        in_specs += [pl.BlockSpec((None, None, 1, 2 * d), lambda i: (layer, row_of(i), 0, 1)),
                     pl.BlockSpec((None, None, 1, 2 * d), lambda i: (layer, row_of(i), 0, 2))]
    if first_layer:
        inputs += [ln_in[0].reshape(1, d), ln_in[1].reshape(1, d)]
        in_specs += [pl.BlockSpec((1, d), lambda i: (0, 0))] * 2
    ng = norm_g.reshape(1, -1)
    inputs += [ng, w_out_b, wg, wu, wd, ln_g, ln_b]
    in_specs += [small(ng), const(w_out_b), const(wg), const(wu), const(wd), small(ln_g), small(ln_b)]
    kern = functools.partial(_outffn_kernel, sample=sample, first_layer=first_layer, d_model=d,
                             n_heads=HGRN_HEADS, alpha=alpha)
    return pl.pallas_call(
        kern,
        grid=(t // tm,),
        in_specs=in_specs,
        out_specs=pl.BlockSpec((tm, d), lambda i: (i, 0)),
        out_shape=jax.ShapeDtypeStruct((t, d), F32),
        scratch_shapes=[pltpu.VMEM((tm, d), F32)],
        compiler_params=_params(("arbitrary",)),
        name="outffn_sample" if sample else "outffn_prompt",
    )(*inputs)


def kernel(x_prompt, x_sample, c_prompt, c_sample, cache_k, cache_v, cache_logf, state_hgrn, state_conv,
           page_table, ln_in_g, ln_in_b, w_mod, b_mod, w_in, conv_w, hgrn_lb_logits, hgrn_norm_g,
           fox_f_bias, w_out, w_ffn_in, w_ffn_out, ln_g, ln_b):
    nbp, seq, d = x_prompt.shape
    nbs, n_steps, _ = x_sample.shape
    depth = w_mod.shape[0]
    alpha = (2 * depth) ** 0.25
    nh, dh = cache_k.shape[3:]
    dff = w_ffn_out.shape[1]
    nchunk = dff // FFN_CHUNK
    hgw = hgrn_lb_logits.shape[1]
    hdk = hgw // HGRN_HEADS

    c_all = jnp.concatenate([c_sample, c_prompt, jnp.zeros((SUBLANES - nbp, d), F32)], axis=0)
    mod = _modulation(c_all, w_mod, b_mod)

    ln_in = (ln_in_g, ln_in_b)
    xp = x_prompt.reshape(nbp * seq, d)
    xs = x_sample.transpose(1, 0, 2).reshape(n_steps * nbs, d)

    ck_t = cache_k.transpose(0, 1, 3, 4, 2)
    cv_t = cache_v.transpose(0, 1, 3, 4, 2)
    clf_t = cache_logf.transpose(0, 1, 3, 2)
    state_t = state_hgrn.transpose(0, 2, 3, 4, 1)

    outs_p = dict(k=[], v=[], lf=[], s=[], conv=[])
    outs_s = dict(k=[], v=[], lf=[], s=[], conv=[])
    for l in range(depth):
        w_in_p = jnp.pad(w_in[l], ((0, 0), (0, LANES - nh))).astype(BF16)
        fbias = jnp.pad(fox_f_bias[l], (0, LANES - nh)).reshape(1, LANES)
        w_out_b = w_out[l].astype(BF16)
        wg = w_ffn_in[l][:, :dff].reshape(d, nchunk, FFN_CHUNK).transpose(1, 0, 2).astype(BF16)
        wu = w_ffn_in[l][:, dff:].reshape(d, nchunk, FFN_CHUNK).transpose(1, 0, 2).astype(BF16)
        wd = w_ffn_out[l].reshape(nchunk, FFN_CHUNK, d).astype(BF16)
        lin = ln_in if l == 0 else None

        (yconv, ulast, qh, kh, lfh, vh, hg, qat, ka, vat, fkt, fvt, lfft, cbase) = _inproj(
            xp, mod, lin, w_in_p, conv_w[l], hgrn_lb_logits, fbias, None, layer=l, n_seq=nbp, sample=False)
        oh, st_t = _hgrn_prompt(qh, kh, lfh, vh, n_seq=nbp)
        yfox = _fox_prompt(ka, qat, vat, cbase, n_seq=nbp, n_heads=nh)
        xp = _outffn(xp, yconv, oh, hg, yfox, mod, lin, hgrn_norm_g[l], w_out_b, wg, wu, wd,
                     ln_g[l], ln_b[l], layer=l, n_seq=nbp, sample=False, alpha=alpha)
        st4 = st_t.reshape(nbp, HGRN_HEADS, hdk, HGRN_HEADS, hdk)
        outs_p["s"].append(jnp.stack([st4[:, hh, :, hh, :] for hh in range(HGRN_HEADS)], axis=1))
        outs_p["k"].append(fkt)
        outs_p["v"].append(fvt)
        outs_p["lf"].append(lfft)
        outs_p["conv"].append(ulast[:, SUBLANES - (CONV_K - 1):, :])

        prev = state_conv[l].transpose(1, 0, 2).reshape((CONV_K - 1) * nbs, -1)
        (yconv, u, qh, kh, lfh, vh, hg, fq, fk, fv, fkt, fvt, lfft) = _inproj(
            xs, mod, lin, w_in_p, conv_w[l], hgrn_lb_logits, fbias, prev, layer=l, n_seq=nbs, sample=True)
        oh, s_fin = _hgrn_sample(qh, kh, lfh, vh, state_t, layer=l)
        yfox = _fox_sample(page_table, ck_t, cv_t, clf_t, fq, fk, fv, lfft, layer=l)
        xs = _outffn(xs, yconv, oh, hg, yfox, mod, lin, hgrn_norm_g[l], w_out_b, wg, wu, wd,
                     ln_g[l], ln_b[l], layer=l, n_seq=nbs, sample=True, alpha=alpha)
        outs_s["s"].append(s_fin)
        outs_s["k"].append(fkt)
        outs_s["v"].append(fvt)
        outs_s["lf"].append(lfft)
        outs_s["conv"].append(u.reshape(n_steps, nbs, -1)[n_steps - (CONV_K - 1):].transpose(1, 0, 2))

    st = lambda xs_: jnp.stack(xs_)
    y_prompt = xp.reshape(nbp, seq, d)
    y_sample = xs.reshape(n_steps, nbs, d).transpose(1, 0, 2)
    kv_p = lambda a: st(a).reshape(depth, nbp, nh, dh, seq).transpose(0, 1, 4, 2, 3)
    kv_s = lambda a: st(a).reshape(depth, n_steps, nh, dh, nbs).transpose(0, 4, 1, 2, 3)
    return (y_prompt, y_sample,
            kv_p(outs_p["k"]), kv_p(outs_p["v"]), st(outs_p["lf"]).transpose(0, 1, 3, 2),
            st(outs_p["s"]), st(outs_p["conv"]),
            kv_s(outs_s["k"]), kv_s(outs_s["v"]), st(outs_s["lf"]).transpose(0, 3, 1, 2),
            st(outs_s["s"]).transpose(0, 4, 1, 2, 3), st(outs_s["conv"]))
```

```python
import functools

import jax
import jax.numpy as jnp
from jax import lax
from jax.experimental import pallas as pl
from jax.experimental.pallas import tpu as pltpu

F32 = jnp.float32
BF16 = jnp.bfloat16

HEAD_DIM = 64
HGRN_HEADS = 4
CONV_K = 3
LN_EPS = 1e-5
RMS_EPS = 1e-6
NEG_BIG = -1e30
LOG2E = 1.4426950408889634

LANES = 128
SUBLANES = 8
VMEM_LIMIT = 56 * 1024 * 1024
TOKEN_TILE = 512
HGRN_CHUNK = 128
MOD_TILE_N = 1536
FFN_CHUNK = 256
N_BIAS_PARTS = 3
FOX_SCORE_BUFFERS = 3

NT_DIMS = (((1,), (1,)), ((), ()))
TN_DIMS = (((0,), (0,)), ((), ()))


def _params(semantics):
    return pltpu.CompilerParams(dimension_semantics=semantics, vmem_limit_bytes=VMEM_LIMIT)


def _layer_norm(x, g, b):
    mu = jnp.mean(x, axis=-1, keepdims=True)
    xc = x - mu
    var = jnp.mean(xc * xc, axis=-1, keepdims=True)
    return xc * lax.rsqrt(var + LN_EPS) * g + b


def _softplus_tail(x):
    return jnp.log(1.0 + jnp.exp(-jnp.abs(x)))


def _log_sigmoid(x):
    return -(jnp.maximum(-x, 0.0) + _softplus_tail(x))


def _split3(x):
    hi = x.astype(BF16)
    r = x - hi.astype(F32)
    mid = r.astype(BF16)
    lo = (r - mid.astype(F32)).astype(BF16)
    return hi, mid, lo


def _dot_exact_lhs(a, parts):
    n = parts[0].shape[1]
    r = jnp.dot(a, jnp.concatenate(parts, axis=1), preferred_element_type=F32)
    return sum(r[:, j * n:(j + 1) * n] for j in range(len(parts)))


def _dot_exact_rhs(parts, a):
    m = parts[0].shape[0]
    r = jnp.dot(jnp.concatenate(parts, axis=0), a, preferred_element_type=F32)
    return sum(r[j * m:(j + 1) * m] for j in range(len(parts)))


def _split2(x):
    hi = x.astype(BF16)
    return hi, (x - hi.astype(F32)).astype(BF16)


def _iota(shape, axis):
    return lax.broadcasted_iota(jnp.int32, shape, axis)


def _log2(n):
    assert n > 0 and n & (n - 1) == 0, n
    return n.bit_length() - 1


def _div(x, n):
    return lax.shift_right_logical(x, _log2(n))


def _mod(x, n):
    return jnp.bitwise_and(x, n - 1)


def _mod_kernel(c_ref, w_ref, b_ref, o_ref):
    c = c_ref[...]
    a = (c * jax.nn.sigmoid(c)).astype(BF16)
    o_ref[...] = jnp.dot(a, w_ref[...].astype(BF16), preferred_element_type=F32) + b_ref[...]


def _modulation(c_all, w_mod, b_mod):
    depth, d, n = w_mod.shape
    rows = c_all.shape[0]
    return pl.pallas_call(
        _mod_kernel,
        grid=(depth, n // MOD_TILE_N),
        in_specs=[
            pl.BlockSpec((rows, d), lambda l, j: (0, 0)),
            pl.BlockSpec((None, d, MOD_TILE_N), lambda l, j: (l, 0, j)),
            pl.BlockSpec((None, 1, MOD_TILE_N), lambda l, j: (l, 0, j)),
        ],
        out_specs=pl.BlockSpec((None, rows, MOD_TILE_N), lambda l, j: (l, 0, j)),
        out_shape=jax.ShapeDtypeStruct((depth, rows, n), F32),
        compiler_params=_params(("arbitrary", "arbitrary")),
        name="modulation",
    )(c_all, w_mod, b_mod.reshape(depth, 1, n))


def _head_rows(x, hh, dh):
    per = LANES // dh
    slab = x[:, (hh // per) * LANES:(hh // per + 1) * LANES]
    shift = (hh % per) * dh
    return pltpu.roll(slab, LANES - shift, 1) if shift else slab


def _inproj_kernel(*refs, layer, sample, first_layer, tiles_per_seq, n_steps, d_model, conv_w, hgrn_w, fox_w):
    it = iter(refs)
    x_ref, mod_ref = next(it), next(it)
    if first_layer:
        lng_ref, lnb_ref = next(it), next(it)
    w_ref, cw_ref, lbl_ref, fb_ref = next(it), next(it), next(it), next(it)
    if sample:
        prev_ref = next(it)
        (yconv_ref, u_ref, qh_ref, kh_ref, lfh_ref, vh_ref, hg_ref,
         fq_ref, fk_ref, fv_ref, fkt_ref, fvt_ref, lfft_ref) = (next(it) for _ in range(13))
    else:
        (yconv_ref, ulast_ref, qh_ref, kh_ref, lfh_ref, vh_ref, hg_ref,
         qat_ref, ka_ref, vat_ref, fkt_ref, fvt_ref, lfft_ref, cbase_ref,
         carry_ref, ccarry_ref) = (next(it) for _ in range(16))

    tm = x_ref.shape[0]
    dh = HEAD_DIM
    nh = fox_w // dh
    x = x_ref[...]
    if first_layer:
        x = _layer_norm(x, lng_ref[...], lnb_ref[...])
    m = mod_ref[...]
    if sample:
        m = jnp.concatenate([m] * (tm // m.shape[0]), axis=0)
    sh1, sc1 = m[:, :d_model], m[:, d_model:]
    h = (x * (1.0 + sc1) + sh1).astype(BF16)

    c0 = 0
    pc = jnp.dot(h, w_ref[:, c0:c0 + 3 * conv_w], preferred_element_type=F32)
    cb, cc, ch = pc[:, :conv_w], pc[:, conv_w:2 * conv_w], pc[:, 2 * conv_w:]
    u = cc * ch
    if sample:
        nb = tm // n_steps
        prev0, prev1 = prev_ref[:nb, :], prev_ref[nb:, :]
        u1 = jnp.concatenate([prev1, u[:tm - nb]], axis=0)
        u2 = jnp.concatenate([prev0, prev1, u[:tm - 2 * nb]], axis=0)
        u_ref[...] = u
    else:
        i = pl.program_id(0)

        @pl.when(i % tiles_per_seq == 0)
        def _():
            carry_ref[...] = jnp.zeros_like(carry_ref)
            ccarry_ref[...] = jnp.zeros_like(ccarry_ref)

        prev = carry_ref[...]
        row = _iota((tm, 1), 0)
        last1, last2 = prev[SUBLANES - 1:SUBLANES], prev[SUBLANES - 2:SUBLANES - 1]
        u1 = jnp.where(row == 0, last1, pltpu.roll(u, 1, 0))
        u2 = jnp.where(row == 0, last2, jnp.where(row == 1, last1, pltpu.roll(u, 2, 0)))
        carry_ref[...] = u[tm - SUBLANES:]
        ulast_ref[...] = u[tm - SUBLANES:]
    cw = cw_ref[...]
    yconv_ref[...] = (cb * (cw[0:1] * u2 + cw[1:2] * u1 + cw[2:3] * u)).astype(BF16)

    c0 += 3 * conv_w
    ph = jnp.dot(h, w_ref[:, c0:c0 + 4 * hgrn_w], preferred_element_type=F32)
    hq, hf = ph[:, :hgrn_w], ph[:, hgrn_w:2 * hgrn_w]
    hi, hg = ph[:, 2 * hgrn_w:3 * hgrn_w], ph[:, 3 * hgrn_w:]
    logits = lbl_ref[...]
    e = jnp.exp(logits - jnp.max(logits, axis=0, keepdims=True))
    p = e / jnp.sum(e, axis=0, keepdims=True)
    csum = p[0:1]
    for j in range(1, layer + 1):
        csum = csum + p[j:j + 1]
    lb = csum - p[layer:layer + 1]
    has_lb = lb > 0
    log_lb = jnp.where(has_lb, jnp.log(jnp.where(has_lb, lb, 1.0)), NEG_BIG)
    b2 = jnp.log1p(-lb) + _log_sigmoid(hf)
    lfh = jnp.maximum(log_lb, b2) + _softplus_tail(log_lb - b2)
    kh = (1.0 - lb) * jax.nn.sigmoid(-hf)
    qh = hq * jax.nn.sigmoid(hq)
    hg_ref[...] = hg
    if sample:
        for s in range(n_steps):
            rows = slice(s * nb, (s + 1) * nb)
            qh_ref[s] = qh[rows].T
            kh_ref[s] = kh[rows].T
            lfh_ref[s] = lfh[rows].T
            vh_ref[s] = hi[rows].T
    else:
        qh_ref[...] = qh
        kh_ref[...] = kh
        lfh_ref[...] = lfh
        vh_ref[...] = hi

    c0 += 4 * hgrn_w
    pf = jnp.dot(h, w_ref[:, c0:c0 + 3 * fox_w], preferred_element_type=F32)
    fq, fk, fv = pf[:, :fox_w], pf[:, fox_w:2 * fox_w], pf[:, 2 * fox_w:]
    c0 += 3 * fox_w
    pff = jnp.dot(h, w_ref[:, c0:c0 + LANES], preferred_element_type=F32)
    lff = _log_sigmoid(pff + fb_ref[...])
    if sample:
        fq_ref[...] = (fq * (dh ** -0.5)).reshape(fq_ref.shape)
        fk_ref[...] = fk.reshape(fk_ref.shape)
        fv_ref[...] = fv.reshape(fv_ref.shape)
        for s in range(n_steps):
            rows = slice(s * nb, (s + 1) * nb)
            fkt_ref[s] = fk[rows].T
            fvt_ref[s] = fv[rows].T
            lfft_ref[s] = lff[rows].T[:nh]
        return

    fvt = fv.T
    fqt = (fq * (LOG2E * dh ** -0.5)).T
    fkt_ref[...] = fk.T
    fvt_ref[...] = fvt
    lfft_ref[...] = lff.T[:nh]
    tri = (_iota((tm, tm), 0) >= _iota((tm, tm), 1)).astype(BF16)
    c_nat = _dot_exact_lhs(tri, _split3(lff))
    c_rel = c_nat * LOG2E
    c_rel_t = c_rel.T
    base = ccarry_ref[...]
    cbase_ref[...] = base * LOG2E
    ccarry_ref[...] = base + jnp.broadcast_to(c_nat[tm - 1:tm], ccarry_ref.shape)

    lane = _iota((tm, LANES), 1)
    srow = _iota((SUBLANES, tm), 0)
    npz = N_BIAS_PARTS
    pad_rows = jnp.zeros((LANES - dh - SUBLANES, tm), F32)
    c_parts = [pp.astype(F32) for pp in _split3(c_rel)]
    for hh in range(nh):
        ck = [jnp.broadcast_to(pp[:, hh:hh + 1], (tm, LANES)) for pp in c_parts]
        extra = jnp.where(lane < dh + npz, 1.0, 0.0)
        for j in range(npz):
            extra = jnp.where(lane == dh + npz + j, -ck[j], extra)
        ka_ref[:, hh * LANES:(hh + 1) * LANES] = jnp.where(lane < dh, _head_rows(fk, hh, dh), extra).astype(BF16)
        cq = [pp.astype(F32) for pp in _split3(c_rel_t[hh:hh + 1, :])]
        grp = jnp.where((srow >= npz) & (srow < 2 * npz), 1.0, 0.0)
        for j in range(npz):
            grp = jnp.where(srow == j, cq[j], grp)
        qat_ref[hh * LANES:(hh + 1) * LANES, :] = jnp.concatenate(
            [fqt[hh * dh:(hh + 1) * dh], grp, pad_rows], axis=0).astype(BF16)
        vat_ref[hh * LANES:(hh + 1) * LANES, :] = jnp.concatenate(
            [fvt[hh * dh:(hh + 1) * dh], jnp.where(srow == 0, 1.0, 0.0), pad_rows], axis=0).astype(BF16)


def _inproj(x, mod, ln_in, w_in_p, conv_w, lb_logits, fbias, prev, *, layer, n_seq, sample):
    t, d = x.shape
    tm = TOKEN_TILE
    n_tiles = t // tm
    seq = t // n_seq
    tiles_per_seq = seq // tm if not sample else 1
    n_steps = t // n_seq if sample else 0
    convw = conv_w.shape[1]
    hgw = lb_logits.shape[1]
    ncols = w_in_p.shape[1]
    foxw = (ncols - LANES - 3 * convw - 4 * hgw) // 3
    nh = foxw // HEAD_DIM
    first_layer = ln_in is not None
    depth = mod.shape[0]

    inputs, in_specs = [x], [pl.BlockSpec((tm, d), lambda i: (i, 0))]
    if sample:
        assert n_tiles == 1
        inputs.append(mod)
        in_specs.append(pl.BlockSpec((None, n_seq, 2 * d), lambda i: (layer, 0, 0)))
    else:
        rows = mod.shape[1]
        inputs.append(mod.reshape(depth, rows, 1, mod.shape[2]))
        in_specs.append(pl.BlockSpec((None, None, 1, 2 * d),
                                     lambda i: (layer, rows - SUBLANES + i // tiles_per_seq, 0, 0)))
    if first_layer:
        inputs += [ln_in[0].reshape(1, d), ln_in[1].reshape(1, d)]
        in_specs += [pl.BlockSpec((1, d), lambda i: (0, 0))] * 2
    inputs += [w_in_p, conv_w, lb_logits, fbias]
    in_specs += [
        pl.BlockSpec((d, ncols), lambda i: (0, 0), pipeline_mode=pl.Buffered(1)),
        pl.BlockSpec(conv_w.shape, lambda i: (0, 0)),
        pl.BlockSpec(lb_logits.shape, lambda i: (0, 0)),
        pl.BlockSpec((1, LANES), lambda i: (0, 0)),
    ]
    tok = lambda w, dt: (jax.ShapeDtypeStruct((t, w), dt), pl.BlockSpec((tm, w), lambda i: (i, 0)))
    whole = lambda shape, dt: (jax.ShapeDtypeStruct(shape, dt), pl.BlockSpec(shape, lambda i: (0,) * len(shape)))
    scratch = []
    if sample:
        inputs.append(prev)
        in_specs.append(pl.BlockSpec(prev.shape, lambda i: (0, 0)))
        outs = ([tok(convw, BF16), tok(convw, F32)] + [whole((n_steps, hgw, n_seq), F32)] * 4 + [tok(hgw, F32)]
                + [whole((n_steps, n_seq, foxw), F32)] * 3 + [whole((n_steps, foxw, n_seq), F32)] * 2
                + [whole((n_steps, nh, n_seq), F32)])
    else:
        per_seq_t = lambda rows, dt: (
            jax.ShapeDtypeStruct((n_seq, rows, seq), dt),
            pl.BlockSpec((None, rows, tm), lambda i: (i // tiles_per_seq, 0, i % tiles_per_seq)))
        outs = ([tok(convw, BF16),
                 (jax.ShapeDtypeStruct((n_seq, SUBLANES, convw), F32),
                  pl.BlockSpec((None, SUBLANES, convw), lambda i: (i // tiles_per_seq, 0, 0)))]
                + [tok(hgw, F32)] * 5
                + [per_seq_t(nh * LANES, BF16), tok(nh * LANES, BF16), per_seq_t(nh * LANES, BF16),
                   per_seq_t(foxw, F32), per_seq_t(foxw, F32), per_seq_t(nh, F32),
                   (jax.ShapeDtypeStruct((n_tiles, SUBLANES, LANES), F32),
                    pl.BlockSpec((None, SUBLANES, LANES), lambda i: (i, 0, 0)))])
        scratch = [pltpu.VMEM((SUBLANES, convw), F32), pltpu.VMEM((SUBLANES, LANES), F32)]
    kern = functools.partial(_inproj_kernel, layer=layer, sample=sample, first_layer=first_layer,
                             tiles_per_seq=tiles_per_seq, n_steps=n_steps, d_model=d, conv_w=convw,
                             hgrn_w=hgw, fox_w=foxw)
    return pl.pallas_call(
        kern,
        grid=(n_tiles,),
        in_specs=in_specs,
        out_specs=[o[1] for o in outs],
        out_shape=[o[0] for o in outs],
        scratch_shapes=scratch,
        compiler_params=_params(("arbitrary",)),
        name="inproj_sample" if sample else "inproj_prompt",
    )(*inputs)


def _hgrn_prompt_kernel(q_ref, k_ref, g_ref, v_ref, o_ref, st_out_ref, st_ref, *, chunk, n_heads):
    i = pl.program_id(1)
    tm, w = q_ref.shape
    dh = w // n_heads
    c = chunk

    @pl.when(i == 0)
    def _():
        st_ref[...] = jnp.zeros_like(st_ref)

    r_cc, c_cc = _iota((c, c), 0), _iota((c, c), 1)
    tri = (r_cc >= c_cc).astype(BF16)
    lane = _iota((1, w), 1)
    head_masks = [_div(lane, dh) == hh for hh in range(n_heads)]
    bd = _div(_iota((w, w), 0), dh) == _div(_iota((w, w), 1), dh)
    bd_bf = bd.astype(BF16)
    row = _iota((c, 1), 0)
    col_key = _mod(_iota((c, n_heads * c), 1), c)
    row_q = _iota((c, n_heads * c), 0)
    levels = []
    b = c // 2
    while b >= 1:
        levels.append(dict(
            half=b,
            upper=_mod(row, 2 * b) >= b,
            lower=_mod(row, 2 * b) < b,
            same=_div(row_q, 2 * b) == _div(col_key, 2 * b),
        ))
        b //= 2
    sub = _iota((1, SUBLANES, 1), 1)

    def boundary_rows(x, b):
        if 2 * b >= SUBLANES:
            return jnp.concatenate(
                [jnp.broadcast_to(x[s0 + b - 1:s0 + b], (2 * b, w)) for s0 in range(0, c, 2 * b)], axis=0)
        x3 = x.reshape(c // SUBLANES, SUBLANES, w)
        out = None
        for s0 in range(SUBLANES - 2 * b, -1, -2 * b):
            rep = jnp.broadcast_to(x3[:, s0 + b - 1:s0 + b, :], x3.shape)
            out = rep if out is None else jnp.where(sub < s0 + 2 * b, rep, out)
        return out.reshape(c, w)

    chunks = range(tm // c)
    rows = [slice(ci * c, (ci + 1) * c) for ci in chunks]
    q = [q_ref[r, :] for r in rows]
    k = [k_ref[r, :] for r in rows]
    v = [v_ref[r, :] for r in rows]
    vb = [x.astype(BF16) for x in v]
    gcum = [_dot_exact_lhs(tri, _split3(g_ref[r, :])) * LOG2E for r in rows]
    a = [jnp.zeros((c, n_heads * c), F32) for _ in chunks]
    for lv in levels:
        for ci in chunks:
            d = gcum[ci] - boundary_rows(gcum[ci], lv["half"])
            e = jnp.exp2(jnp.where(lv["upper"], d, -d))
            qt = jnp.where(lv["upper"], q[ci] * e, 0.0).astype(BF16)
            kt = jnp.where(lv["lower"], k[ci] * e, 0.0).astype(BF16)
            kbd = jnp.concatenate([jnp.where(hm, kt, jnp.zeros_like(kt)) for hm in head_masks], axis=0)
            xl = lax.dot_general(qt, kbd, NT_DIMS, preferred_element_type=F32)
            a[ci] = jnp.where(lv["same"], xl, a[ci])
    o = []
    for ci in chunks:
        diag = _dot_exact_rhs(_split2(q[ci] * k[ci]), bd_bf)
        vbd = jnp.concatenate([jnp.where(hm, vb[ci], jnp.zeros_like(vb[ci])) for hm in head_masks], axis=0)
        o.append(diag * v[ci] + jnp.dot(a[ci].astype(BF16), vbd, preferred_element_type=F32))
    st = st_ref[...]
    for ci in chunks:
        qg = (q[ci] * jnp.exp2(gcum[ci])).astype(BF16)
        o_ref[rows[ci], :] = o[ci] + lax.dot_general(qg, st.astype(BF16), NT_DIMS, preferred_element_type=F32)
        g_last = gcum[ci][c - 1:c]
        kd = (k[ci] * jnp.exp2(g_last - gcum[ci])).astype(BF16)
        upd = lax.dot_general(vb[ci], kd, TN_DIMS, preferred_element_type=F32)
        st = st * jnp.exp2(g_last) + jnp.where(bd, upd, 0.0)
    st_ref[...] = st

    @pl.when(i == pl.num_programs(1) - 1)
    def _():
        st_out_ref[...] = st_ref[...].T


def _hgrn_prompt(qh, kh, lfh, vh, *, n_seq):
    t, w = qh.shape
    tm = TOKEN_TILE
    tps = (t // n_seq) // tm
    spec = pl.BlockSpec((tm, w), lambda b, i: (b * tps + i, 0))
    kern = functools.partial(_hgrn_prompt_kernel, chunk=HGRN_CHUNK, n_heads=HGRN_HEADS)
    return pl.pallas_call(
        kern,
        grid=(n_seq, tps),
        in_specs=[spec] * 4,
        out_specs=[spec, pl.BlockSpec((None, w, w), lambda b, i: (b, 0, 0))],
        out_shape=[jax.ShapeDtypeStruct((t, w), F32), jax.ShapeDtypeStruct((n_seq, w, w), F32)],
        scratch_shapes=[pltpu.VMEM((w, w), F32)],
        compiler_params=_params(("arbitrary", "arbitrary")),
        name="hgrn_prompt",
    )(qh, kh, lfh, vh)


def _hgrn_sample_kernel(q_ref, k_ref, g_ref, v_ref, s0_ref, o_ref, s_out_ref):
    n_steps, dh, nb = q_ref.shape

    def body(d, o_acc):
        s = s0_ref[d]
        out = []
        for t in range(n_steps):
            row = pl.ds(d, 1)
            s = jnp.exp(g_ref[t, row, :]) * s + k_ref[t, row, :] * v_ref[t]
            out.append(o_acc[t] + q_ref[t, row, :] * s)
        s_out_ref[d] = s
        return tuple(out)

    o = lax.fori_loop(0, dh, body, tuple(jnp.zeros((v_ref.shape[1], nb), F32) for _ in range(n_steps)))
    for t in range(n_steps):
        o_ref[t] = o[t]


def _hgrn_sample(qh_t, kh_t, lfh_t, vh_t, state_t, *, layer):
    n_steps, w, nb = qh_t.shape
    _, n_heads, dk, dv, _ = state_t.shape
    gate = pl.BlockSpec((n_steps, dk, nb), lambda h: (0, h, 0))
    val = pl.BlockSpec((n_steps, dv, nb), lambda h: (0, h, 0))
    return pl.pallas_call(
        _hgrn_sample_kernel,
        grid=(n_heads,),
        in_specs=[gate, gate, gate, val,
                  pl.BlockSpec((None, None, dk, dv, nb), lambda h: (layer, h, 0, 0, 0))],
        out_specs=[val, pl.BlockSpec((None, dk, dv, nb), lambda h: (h, 0, 0, 0))],
        out_shape=[jax.ShapeDtypeStruct((n_steps, n_heads * dv, nb), F32),
                   jax.ShapeDtypeStruct((n_heads, dk, dv, nb), F32)],
        compiler_params=_params(("arbitrary",)),
        name="hgrn_sample",
    )(qh_t, kh_t, lfh_t, vh_t, state_t)


def _fox_prompt_kernel(qi_ref, kj_ref, last_ref, ka_ref, qat_ref, vat_ref, cbq_ref, cbk_ref, o_ref,
                       m_scr, acc_scr, s_scr, mask_scr):
    pair = pl.program_id(1)
    qi, kj = qi_ref[pair], kj_ref[pair]
    tk, tq = ka_ref.shape[0], qat_ref.shape[1]
    nh = m_scr.shape[0]
    dh = HEAD_DIM

    @pl.when((pl.program_id(0) == 0) & (pair == 0))
    def _():
        mask_scr[...] = jnp.where(_iota((tk, tq), 0) <= _iota((tk, tq), 1), 0.0, NEG_BIG)

    @pl.when(kj == 0)
    def _():
        m_scr[...] = jnp.full_like(m_scr, NEG_BIG)
        acc_scr[...] = jnp.zeros_like(acc_scr)

    delta = cbq_ref[0:1, :] - cbk_ref[0:1, :]

    nbuf = s_scr.shape[0]

    def tile(masked):
        def scores(hh):
            ls = slice(hh * LANES, (hh + 1) * LANES)
            s = jnp.dot(ka_ref[:, ls], qat_ref[ls, :], preferred_element_type=F32)
            if masked:
                s = s + mask_scr[...]
            s_scr[hh % nbuf] = s

        for hh in range(min(nbuf - 1, nh)):
            scores(hh)
        for hh in range(nh):
            if hh + nbuf - 1 < nh:
                scores(hh + nbuf - 1)
            ls = slice(hh * LANES, (hh + 1) * LANES)
            s = s_scr[hh % nbuf]
            d = delta[:, hh:hh + 1]
            m_prev = m_scr[hh]
            m_next = jnp.maximum(m_prev, jnp.max(s, axis=0, keepdims=True) + d)
            p = jnp.exp2(s - (m_next[0:1] - d))
            alpha = jnp.exp2(m_prev - m_next)
            pv = jnp.dot(vat_ref[ls, :], p.astype(BF16), preferred_element_type=F32)
            acc_scr[hh] = acc_scr[hh] * alpha[0:1] + pv
            m_scr[hh] = m_next

    pl.when(kj < qi)(lambda: tile(False))
    pl.when(kj == qi)(lambda: tile(True))

    @pl.when(last_ref[pair] == 1)
    def _():
        per = LANES // dh
        for slab in range(nh // per):
            vals = []
            for j in range(per):
                acc = acc_scr[slab * per + j]
                vals.append(acc[:dh] / acc[dh:dh + 1])
            o_ref[:, slab * LANES:(slab + 1) * LANES] = jnp.concatenate(vals, axis=0).T.astype(BF16)


def _fox_prompt(ka, qat, vat, cbase, *, n_seq, n_heads):
    t, wa = ka.shape
    tq = tk = TOKEN_TILE
    nq = (t // n_seq) // tq
    pairs = [(i, j) for i in range(nq) for j in range(i + 1)]
    qi_tab = jnp.asarray([p[0] for p in pairs], jnp.int32)
    kj_tab = jnp.asarray([p[1] for p in pairs], jnp.int32)
    last_tab = jnp.asarray([int(p[0] == p[1]) for p in pairs], jnp.int32)
    grid_spec = pltpu.PrefetchScalarGridSpec(
        num_scalar_prefetch=3,
        grid=(n_seq, len(pairs)),
        in_specs=[
            pl.BlockSpec((tk, wa), lambda b, p, qi, kj, last: (b * nq + kj[p], 0)),
            pl.BlockSpec((None, wa, tq), lambda b, p, qi, kj, last: (b, 0, qi[p])),
            pl.BlockSpec((None, wa, tk), lambda b, p, qi, kj, last: (b, 0, kj[p])),
            pl.BlockSpec((None, SUBLANES, LANES), lambda b, p, qi, kj, last: (b * nq + qi[p], 0, 0)),
            pl.BlockSpec((None, SUBLANES, LANES), lambda b, p, qi, kj, last: (b * nq + kj[p], 0, 0)),
        ],
        out_specs=pl.BlockSpec((tq, n_heads * HEAD_DIM), lambda b, p, qi, kj, last: (b * nq + qi[p], 0)),
        scratch_shapes=[pltpu.VMEM((n_heads, SUBLANES, tq), F32), pltpu.VMEM((n_heads, LANES, tq), F32),
                        pltpu.VMEM((FOX_SCORE_BUFFERS, tk, tq), F32), pltpu.VMEM((tk, tq), F32)],
    )
    return pl.pallas_call(
        _fox_prompt_kernel,
        grid_spec=grid_spec,
        out_shape=jax.ShapeDtypeStruct((t, n_heads * HEAD_DIM), BF16),
        compiler_params=_params(("arbitrary", "arbitrary")),
        name="fox_prompt",
    )(qi_tab, kj_tab, last_tab, ka, qat, vat, cbase, cbase)


def _fox_sample_kernel(pt_ref, *refs, n_pages, n_steps, n_heads):
    del pt_ref
    kp = refs[:n_pages]
    vp = refs[n_pages:2 * n_pages]
    lp = refs[2 * n_pages:3 * n_pages]
    q_ref, kn_ref, vn_ref, lfn_ref, o_ref, knew_scr, vnew_scr = refs[3 * n_pages:]
    _, dh, page = kp[0].shape
    w = n_heads * dh
    rows = n_steps * n_heads
    b = pl.program_id(0)
    r = pl.ds(_mod(b, SUBLANES), 1)

    @pl.when(b == 0)
    def _():
        knew_scr[...] = jnp.zeros_like(knew_scr)
        vnew_scr[...] = jnp.zeros_like(vnew_scr)

    hmask = _div(_iota((n_heads, w), 1), dh) == _iota((n_heads, w), 0)
    qbd = []
    for s in range(n_steps):
        knew_scr[s:s + 1, :] = kn_ref[s, r, :]
        vnew_scr[s:s + 1, :] = vn_ref[s, r, :]
        qbd.append(jnp.where(hmask, jnp.broadcast_to(q_ref[s, r, :], (n_heads, w)), 0.0))
    qbd = jnp.concatenate(qbd, axis=0).astype(BF16)

    lf_all = jnp.concatenate([lp[j][...] for j in range(n_pages)], axis=0)
    parts = _split3(lf_all)
    after = (_iota((page, page), 0) > _iota((page, page), 1)).astype(BF16)
    ones = jnp.ones((page, page), BF16)
    excl = sum(jnp.dot(pp, after, preferred_element_type=F32) for pp in parts)
    total = sum(jnp.dot(pp, ones, preferred_element_type=F32) for pp in parts)
    scores = [None] * n_pages
    later = jnp.zeros((n_heads, page), F32)
    for j in range(n_pages - 1, -1, -1):
        ps = slice(j * n_heads, (j + 1) * n_heads)
        bias = excl[ps] + later
        later = later + total[ps]
        kt = kp[j][...].reshape(w, page).astype(BF16)
        s = jnp.dot(qbd, kt, preferred_element_type=F32)
        scores[j] = s + jnp.concatenate([bias] * n_steps, axis=0)
    lane_h = _iota((n_heads, page), 1)
    seq_lane = _iota(lfn_ref.shape[1:], 1) == b
    bias_new = jnp.zeros((n_heads, page), F32)
    cum = jnp.zeros((n_heads, 1), F32)
    for s in range(n_steps):
        cum = cum + jnp.sum(jnp.where(seq_lane, lfn_ref[s], 0.0), axis=1, keepdims=True)
        bias_new = jnp.where(lane_h == s, -cum, bias_new)
    s_new = lax.dot_general(qbd, knew_scr[...].astype(BF16), NT_DIMS, preferred_element_type=F32)
    s_new = s_new + jnp.concatenate([bias_new] * n_steps, axis=0)
    key_pos = _iota((rows, page), 1)
    q_step = _div(_iota((rows, page), 0), n_heads)
    scores.append(jnp.where(key_pos <= q_step, s_new, NEG_BIG))

    m = scores[0].max(axis=1, keepdims=True)
    for s in scores[1:]:
        m = jnp.maximum(m, s.max(axis=1, keepdims=True))
    acc = jnp.zeros((rows, w), F32)
    den = jnp.zeros((rows, 1), F32)
    for j, s in enumerate(scores):
        p = jnp.exp(s - m)
        den = den + p.sum(axis=1, keepdims=True)
        pb = p.astype(BF16)
        if j < n_pages:
            vt = vp[j][...].reshape(w, page).astype(BF16)
            acc = acc + lax.dot_general(pb, vt, NT_DIMS, preferred_element_type=F32)
        else:
            acc = acc + jnp.dot(pb, vnew_scr[...].astype(BF16), preferred_element_type=F32)
    out = acc / den
    for s in range(n_steps):
        blk = jnp.where(hmask, out[s * n_heads:(s + 1) * n_heads], 0.0)
        o_ref[s, r, :] = blk.sum(axis=0, keepdims=True)


def _fox_sample(page_table, ck_t, cv_t, clf_t, fq, fk, fv, lff_t, *, layer):
    nb, n_pages = page_table.shape
    _, n_phys, n_heads, dh, page = ck_t.shape
    n_steps = fq.shape[0]
    w = n_heads * dh
    pt_flat = page_table.reshape(-1)

    def page_spec(shape, j):
        nd = len(shape)
        return pl.BlockSpec((None, None) + shape, lambda b, pt: (layer, pt[b * n_pages + j]) + (0,) * nd)

    in_specs = ([page_spec((n_heads, dh, page), j) for j in range(n_pages)] * 2
                + [page_spec((n_heads, page), j) for j in range(n_pages)])
    step_spec = pl.BlockSpec((n_steps, SUBLANES, w), lambda b, pt: (0, b // SUBLANES, 0))
    in_specs += [step_spec] * 3 + [pl.BlockSpec(lff_t.shape, lambda b, pt: (0, 0, 0))]
    kern = functools.partial(_fox_sample_kernel, n_pages=n_pages, n_steps=n_steps, n_heads=n_heads)
    grid_spec = pltpu.PrefetchScalarGridSpec(
        num_scalar_prefetch=1,
        grid=(nb,),
        in_specs=in_specs,
        out_specs=step_spec,
        scratch_shapes=[pltpu.VMEM((page, w), F32), pltpu.VMEM((page, w), F32)],
    )
    return pl.pallas_call(
        kern,
        grid_spec=grid_spec,
        out_shape=jax.ShapeDtypeStruct((n_steps, nb, w), F32),
        compiler_params=_params(("arbitrary",)),
        name="fox_sample",
    )(pt_flat, *([ck_t] * n_pages), *([cv_t] * n_pages), *([clf_t] * n_pages), fq, fk, fv, lff_t)


def _outffn_kernel(*refs, sample, first_layer, d_model, n_heads, alpha):
    it = iter(refs)
    x_ref, yc_ref, oh_ref, hg_ref, yf_ref, ma_ref, mb_ref = (next(it) for _ in range(7))
    if first_layer:
        lng_ref, lnb_ref = next(it), next(it)
    ng_ref, wo_ref, wg_ref, wu_ref, wd_ref, lg_ref, lb_ref, out_ref, acc_ref = (next(it) for _ in range(9))

    tm = x_ref.shape[0]
    x = x_ref[...]
    if first_layer:
        x = _layer_norm(x, lng_ref[...], lnb_ref[...])
    ma, mb = ma_ref[...], mb_ref[...]
    if sample:
        reps = tm // ma.shape[0]
        ma = jnp.concatenate([ma] * reps, axis=0)
        mb = jnp.concatenate([mb] * reps, axis=0)
        o = jnp.concatenate([oh_ref[s].T for s in range(oh_ref.shape[0])], axis=0)
        yf = yf_ref[...].reshape(tm, yf_ref.shape[2]).astype(BF16)
    else:
        o = oh_ref[...]
        yf = yf_ref[...]
    g1, sh2 = ma[:, :d_model], ma[:, d_model:]
    sc2, g2 = mb[:, :d_model], mb[:, d_model:]

    w = o.shape[1]
    dh = w // n_heads
    bd = (_div(_iota((w, w), 0), dh) == _div(_iota((w, w), 1), dh)).astype(BF16)
    ms = _dot_exact_rhs(_split2(o * o), bd) * (1.0 / dh)
    hg = hg_ref[...]
    yh = (o * lax.rsqrt(ms + RMS_EPS) * ng_ref[...] * (hg * jax.nn.sigmoid(hg))).astype(BF16)

    mixed = jnp.concatenate([yc_ref[...], yh, yf], axis=1)
    mix = jnp.dot(mixed, wo_ref[...], preferred_element_type=F32)
    x1 = _layer_norm(alpha * x + (1.0 + g1) * mix, lg_ref[0:1, :], lb_ref[0:1, :])

    h2 = (x1 * (1.0 + sc2) + sh2).astype(BF16)
    acc_ref[...] = jnp.zeros_like(acc_ref)

    def ffn_chunk(ci, carry):
        gate = jnp.dot(h2, wg_ref[ci], preferred_element_type=F32)
        up = jnp.dot(h2, wu_ref[ci], preferred_element_type=F32)
        act = (gate * jax.nn.sigmoid(gate) * up).astype(BF16)
        acc_ref[...] += jnp.dot(act, wd_ref[ci], preferred_element_type=F32)
        return carry

    lax.fori_loop(0, wg_ref.shape[0], ffn_chunk, 0)
    out_ref[...] = _layer_norm(alpha * x1 + (1.0 + g2) * acc_ref[...], lg_ref[1:2, :], lb_ref[1:2, :])


def _outffn(x, yconv, oh, hg, yfox, mod, ln_in, norm_g, w_out_b, wg, wu, wd, ln_g, ln_b,
            *, layer, n_seq, sample, alpha):
    t, d = x.shape
    tm = TOKEN_TILE
    first_layer = ln_in is not None
    depth, rows, _ = mod.shape
    tiles_per_seq = (t // n_seq) // tm if not sample else 1
    tok = lambda a: pl.BlockSpec((tm, a.shape[1]), lambda i: (i, 0))
    const = lambda a: pl.BlockSpec(a.shape, lambda i: (0,) * a.ndim, pipeline_mode=pl.Buffered(1))
    small = lambda a: pl.BlockSpec(a.shape, lambda i: (0,) * a.ndim)

    inputs = [x, yconv, oh, hg, yfox]
    if sample:
        in_specs = [tok(x), tok(yconv), small(oh), tok(hg), small(yfox)]
        inputs += [mod, mod]
        in_specs += [pl.BlockSpec((None, n_seq, 2 * d), lambda i: (layer, 0, 1)),
                     pl.BlockSpec((None, n_seq, 2 * d), lambda i: (layer, 0, 2))]
    else:
        in_specs = [tok(a) for a in inputs]
        m4 = mod.reshape(depth, rows, 1, mod.shape[2])
        inputs += [m4, m4]
        row_of = lambda i: rows - SUBLANES + i // tiles_per_seq
        in_specs += [pl.BlockSpec((None, None, 1, 2 * d), lambda i: (layer, row_of(i), 0, 1)),
                     pl.BlockSpec((None, None, 1, 2 * d), lambda i: (layer, row_of(i), 0, 2))]
    if first_layer:
        inputs += [ln_in[0].reshape(1, d), ln_in[1].reshape(1, d)]
        in_specs += [pl.BlockSpec((1, d), lambda i: (0, 0))] * 2
    ng = norm_g.reshape(1, -1)
    inputs += [ng, w_out_b, wg, wu, wd, ln_g, ln_b]
    in_specs += [small(ng), const(w_out_b), const(wg), const(wu), const(wd), small(ln_g), small(ln_b)]
    kern = functools.partial(_outffn_kernel, sample=sample, first_layer=first_layer, d_model=d,
                             n_heads=HGRN_HEADS, alpha=alpha)
    return pl.pallas_call(
        kern,
        grid=(t // tm,),
        in_specs=in_specs,
        out_specs=pl.BlockSpec((tm, d), lambda i: (i, 0)),
        out_shape=jax.ShapeDtypeStruct((t, d), F32),
        scratch_shapes=[pltpu.VMEM((tm, d), F32)],
        compiler_params=_params(("arbitrary",)),
        name="outffn_sample" if sample else "outffn_prompt",
    )(*inputs)


def kernel(x_prompt, x_sample, c_prompt, c_sample, cache_k, cache_v, cache_logf, state_hgrn, state_conv,
           page_table, ln_in_g, ln_in_b, w_mod, b_mod, w_in, conv_w, hgrn_lb_logits, hgrn_norm_g,
           fox_f_bias, w_out, w_ffn_in, w_ffn_out, ln_g, ln_b):
    nbp, seq, d = x_prompt.shape
    nbs, n_steps, _ = x_sample.shape
    depth = w_mod.shape[0]
    alpha = (2 * depth) ** 0.25
    nh, dh = cache_k.shape[3:]
    dff = w_ffn_out.shape[1]
    nchunk = dff // FFN_CHUNK
    hgw = hgrn_lb_logits.shape[1]
    hdk = hgw // HGRN_HEADS

    c_all = jnp.concatenate([c_sample, c_prompt, jnp.zeros((SUBLANES - nbp, d), F32)], axis=0)
    mod = _modulation(c_all, w_mod, b_mod)

    ln_in = (ln_in_g, ln_in_b)
    xp = x_prompt.reshape(nbp * seq, d)
    xs = x_sample.transpose(1, 0, 2).reshape(n_steps * nbs, d)

    ck_t = cache_k.transpose(0, 1, 3, 4, 2)
    cv_t = cache_v.transpose(0, 1, 3, 4, 2)
    clf_t = cache_logf.transpose(0, 1, 3, 2)
    state_t = state_hgrn.transpose(0, 2, 3, 4, 1)

    outs_p = dict(k=[], v=[], lf=[], s=[], conv=[])
    outs_s = dict(k=[], v=[], lf=[], s=[], conv=[])
    for l in range(depth):
        w_in_p = jnp.pad(w_in[l], ((0, 0), (0, LANES - nh))).astype(BF16)
        fbias = jnp.pad(fox_f_bias[l], (0, LANES - nh)).reshape(1, LANES)
        w_out_b = w_out[l].astype(BF16)
        wg = w_ffn_in[l][:, :dff].reshape(d, nchunk, FFN_CHUNK).transpose(1, 0, 2).astype(BF16)
        wu = w_ffn_in[l][:, dff:].reshape(d, nchunk, FFN_CHUNK).transpose(1, 0, 2).astype(BF16)
        wd = w_ffn_out[l].reshape(nchunk, FFN_CHUNK, d).astype(BF16)
        lin = ln_in if l == 0 else None

        (yconv, ulast, qh, kh, lfh, vh, hg, qat, ka, vat, fkt, fvt, lfft, cbase) = _inproj(
            xp, mod, lin, w_in_p, conv_w[l], hgrn_lb_logits, fbias, None, layer=l, n_seq=nbp, sample=False)
        oh, st_t = _hgrn_prompt(qh, kh, lfh, vh, n_seq=nbp)
        yfox = _fox_prompt(ka, qat, vat, cbase, n_seq=nbp, n_heads=nh)
        xp = _outffn(xp, yconv, oh, hg, yfox, mod, lin, hgrn_norm_g[l], w_out_b, wg, wu, wd,
                     ln_g[l], ln_b[l], layer=l, n_seq=nbp, sample=False, alpha=alpha)
        st4 = st_t.reshape(nbp, HGRN_HEADS, hdk, HGRN_HEADS, hdk)
        outs_p["s"].append(jnp.stack([st4[:, hh, :, hh, :] for hh in range(HGRN_HEADS)], axis=1))
        outs_p["k"].append(fkt)
        outs_p["v"].append(fvt)
        outs_p["lf"].append(lfft)
        outs_p["conv"].append(ulast[:, SUBLANES - (CONV_K - 1):, :])

        prev = state_conv[l].transpose(1, 0, 2).reshape((CONV_K - 1) * nbs, -1)
        (yconv, u, qh, kh, lfh, vh, hg, fq, fk, fv, fkt, fvt, lfft) = _inproj(
            xs, mod, lin, w_in_p, conv_w[l], hgrn_lb_logits, fbias, prev, layer=l, n_seq=nbs, sample=True)
        oh, s_fin = _hgrn_sample(qh, kh, lfh, vh, state_t, layer=l)
        yfox = _fox_sample(page_table, ck_t, cv_t, clf_t, fq, fk, fv, lfft, layer=l)
        xs = _outffn(xs, yconv, oh, hg, yfox, mod, lin, hgrn_norm_g[l], w_out_b, wg, wu, wd,
                     ln_g[l], ln_b[l], layer=l, n_seq=nbs, sample=True, alpha=alpha)
        outs_s["s"].append(s_fin)
        outs_s["k"].append(fkt)
        outs_s["v"].append(fvt)
        outs_s["lf"].append(lfft)
        outs_s["conv"].append(u.reshape(n_steps, nbs, -1)[n_steps - (CONV_K - 1):].transpose(1, 0, 2))

    st = lambda xs_: jnp.stack(xs_)
    y_prompt = xp.reshape(nbp, seq, d)
    y_sample = xs.reshape(n_steps, nbs, d).transpose(1, 0, 2)
    kv_p = lambda a: st(a).reshape(depth, nbp, nh, dh, seq).transpose(0, 1, 4, 2, 3)
    kv_s = lambda a: st(a).reshape(depth, n_steps, nh, dh, nbs).transpose(0, 4, 1, 2, 3)
    return (y_prompt, y_sample,
            kv_p(outs_p["k"]), kv_p(outs_p["v"]), st(outs_p["lf"]).transpose(0, 1, 3, 2),
            st(outs_p["s"]), st(outs_p["conv"]),
            kv_s(outs_s["k"]), kv_s(outs_s["v"]), st(outs_s["lf"]).transpose(0, 3, 1, 2),
            st(outs_s["s"]).transpose(0, 4, 1, 2, 3), st(outs_s["conv"]))
```

```python
import functools

import jax
import jax.numpy as jnp
from jax import lax
from jax.experimental import pallas as pl
from jax.experimental.pallas import tpu as pltpu

F32 = jnp.float32
BF16 = jnp.bfloat16

HEAD_DIM = 64
HGRN_HEADS = 4
CONV_K = 3
LN_EPS = 1e-5
RMS_EPS = 1e-6
NEG_BIG = -1e30
LOG2E = 1.4426950408889634

LANES = 128
SUBLANES = 8
VMEM_LIMIT = 56 * 1024 * 1024
TOKEN_TILE = 512
OUTFFN_TILE = 1024
HGRN_CHUNK = 128
HGRN_TILE = 1024
MOD_TILE_N = 1536
FFN_CHUNK = 256
N_BIAS_PARTS = 3
FOX_SCORE_BUFFERS = 3

NT_DIMS = (((1,), (1,)), ((), ()))
TN_DIMS = (((0,), (0,)), ((), ()))


def _params(semantics):
    return pltpu.CompilerParams(dimension_semantics=semantics, vmem_limit_bytes=VMEM_LIMIT)


def _layer_norm(x, g, b):
    mu = jnp.mean(x, axis=-1, keepdims=True)
    xc = x - mu
    var = jnp.mean(xc * xc, axis=-1, keepdims=True)
    return xc * lax.rsqrt(var + LN_EPS) * g + b


def _softplus_tail(x):
    return jnp.log(1.0 + jnp.exp(-jnp.abs(x)))


def _log_sigmoid(x):
    return -(jnp.maximum(-x, 0.0) + _softplus_tail(x))


def _split3(x):
    hi = x.astype(BF16)
    r = x - hi.astype(F32)
    mid = r.astype(BF16)
    lo = (r - mid.astype(F32)).astype(BF16)
    return hi, mid, lo


def _dot_exact_lhs(a, parts):
    n = parts[0].shape[1]
    r = jnp.dot(a, jnp.concatenate(parts, axis=1), preferred_element_type=F32)
    return sum(r[:, j * n:(j + 1) * n] for j in range(len(parts)))


def _dot_exact_rhs(parts, a):
    m = parts[0].shape[0]
    r = jnp.dot(jnp.concatenate(parts, axis=0), a, preferred_element_type=F32)
    return sum(r[j * m:(j + 1) * m] for j in range(len(parts)))


def _split2(x):
    hi = x.astype(BF16)
    return hi, (x - hi.astype(F32)).astype(BF16)


def _iota(shape, axis):
    return lax.broadcasted_iota(jnp.int32, shape, axis)


def _log2(n):
    assert n > 0 and n & (n - 1) == 0, n
    return n.bit_length() - 1


def _div(x, n):
    return lax.shift_right_logical(x, _log2(n))


def _mod(x, n):
    return jnp.bitwise_and(x, n - 1)


def _mod_kernel(c_ref, w_ref, b_ref, o_ref):
    c = c_ref[...]
    a = (c * jax.nn.sigmoid(c)).astype(BF16)
    o_ref[...] = jnp.dot(a, w_ref[...].astype(BF16), preferred_element_type=F32) + b_ref[...]


def _modulation(c_all, w_mod, b_mod):
    depth, d, n = w_mod.shape
    rows = c_all.shape[0]
    return pl.pallas_call(
        _mod_kernel,
        grid=(depth, n // MOD_TILE_N),
        in_specs=[
            pl.BlockSpec((rows, d), lambda l, j: (0, 0)),
            pl.BlockSpec((None, d, MOD_TILE_N), lambda l, j: (l, 0, j)),
            pl.BlockSpec((None, 1, MOD_TILE_N), lambda l, j: (l, 0, j)),
        ],
        out_specs=pl.BlockSpec((None, rows, MOD_TILE_N), lambda l, j: (l, 0, j)),
        out_shape=jax.ShapeDtypeStruct((depth, rows, n), F32),
        compiler_params=_params(("arbitrary", "arbitrary")),
        name="modulation",
    )(c_all, w_mod, b_mod.reshape(depth, 1, n))


def _head_rows(x, hh, dh):
    per = LANES // dh
    slab = x[:, (hh // per) * LANES:(hh // per + 1) * LANES]
    shift = (hh % per) * dh
    return pltpu.roll(slab, LANES - shift, 1) if shift else slab


def _inproj_kernel(*refs, layer, sample, first_layer, tiles_per_seq, n_steps, d_model, conv_w, hgrn_w, fox_w):
    it = iter(refs)
    x_ref, mod_ref = next(it), next(it)
    if first_layer:
        lng_ref, lnb_ref = next(it), next(it)
    w_ref, cw_ref, lbl_ref, fb_ref = next(it), next(it), next(it), next(it)
    if sample:
        prev_ref = next(it)
        (yconv_ref, u_ref, qh_ref, kh_ref, lfh_ref, vh_ref, hg_ref,
         fq_ref, fk_ref, fv_ref, fkt_ref, fvt_ref, lfft_ref) = (next(it) for _ in range(13))
    else:
        (yconv_ref, ulast_ref, qh_ref, kh_ref, lfh_ref, vh_ref, hg_ref,
         qat_ref, ka_ref, vat_ref, fkt_ref, fvt_ref, lfft_ref, cbase_ref,
         carry_ref, ccarry_ref) = (next(it) for _ in range(16))

    tm = x_ref.shape[0]
    dh = HEAD_DIM
    nh = fox_w // dh
    x = x_ref[...]
    if first_layer:
        x = _layer_norm(x, lng_ref[...], lnb_ref[...])
    m = mod_ref[...]
    if sample:
        m = jnp.concatenate([m] * (tm // m.shape[0]), axis=0)
    sh1, sc1 = m[:, :d_model], m[:, d_model:]
    h = (x * (1.0 + sc1) + sh1).astype(BF16)

    c0 = 0
    pc = jnp.dot(h, w_ref[:, c0:c0 + 3 * conv_w], preferred_element_type=F32)
    cb, cc, ch = pc[:, :conv_w], pc[:, conv_w:2 * conv_w], pc[:, 2 * conv_w:]
    u = cc * ch
    if sample:
        nb = tm // n_steps
        prev0, prev1 = prev_ref[:nb, :], prev_ref[nb:, :]
        u1 = jnp.concatenate([prev1, u[:tm - nb]], axis=0)
        u2 = jnp.concatenate([prev0, prev1, u[:tm - 2 * nb]], axis=0)
        u_ref[...] = u
    else:
        i = pl.program_id(0)

        @pl.when(i % tiles_per_seq == 0)
        def _():
            carry_ref[...] = jnp.zeros_like(carry_ref)
            ccarry_ref[...] = jnp.zeros_like(ccarry_ref)

        prev = carry_ref[...]
        row = _iota((tm, 1), 0)
        last1, last2 = prev[SUBLANES - 1:SUBLANES], prev[SUBLANES - 2:SUBLANES - 1]
        u1 = jnp.where(row == 0, last1, pltpu.roll(u, 1, 0))
        u2 = jnp.where(row == 0, last2, jnp.where(row == 1, last1, pltpu.roll(u, 2, 0)))
        carry_ref[...] = u[tm - SUBLANES:]
        ulast_ref[...] = u[tm - SUBLANES:]
    cw = cw_ref[...]
    yconv_ref[...] = (cb * (cw[0:1] * u2 + cw[1:2] * u1 + cw[2:3] * u)).astype(BF16)

    c0 += 3 * conv_w
    ph = jnp.dot(h, w_ref[:, c0:c0 + 4 * hgrn_w], preferred_element_type=F32)
    hq, hf = ph[:, :hgrn_w], ph[:, hgrn_w:2 * hgrn_w]
    hi, hg = ph[:, 2 * hgrn_w:3 * hgrn_w], ph[:, 3 * hgrn_w:]
    logits = lbl_ref[...]
    e = jnp.exp(logits - jnp.max(logits, axis=0, keepdims=True))
    p = e / jnp.sum(e, axis=0, keepdims=True)
    csum = p[0:1]
    for j in range(1, layer + 1):
        csum = csum + p[j:j + 1]
    lb = csum - p[layer:layer + 1]
    has_lb = lb > 0
    log_lb = jnp.where(has_lb, jnp.log(jnp.where(has_lb, lb, 1.0)), NEG_BIG)
    b2 = jnp.log1p(-lb) + _log_sigmoid(hf)
    lfh = jnp.maximum(log_lb, b2) + _softplus_tail(log_lb - b2)
    kh = (1.0 - lb) * jax.nn.sigmoid(-hf)
    qh = hq * jax.nn.sigmoid(hq)
    hg_ref[...] = hg
    if sample:
        for s in range(n_steps):
            rows = slice(s * nb, (s + 1) * nb)
            qh_ref[s] = qh[rows].T
            kh_ref[s] = kh[rows].T
            lfh_ref[s] = lfh[rows].T
            vh_ref[s] = hi[rows].T
    else:
        qh_ref[...] = qh
        kh_ref[...] = kh
        lfh_ref[...] = lfh
        vh_ref[...] = hi

    c0 += 4 * hgrn_w
    pf = jnp.dot(h, w_ref[:, c0:c0 + 3 * fox_w], preferred_element_type=F32)
    fq, fk, fv = pf[:, :fox_w], pf[:, fox_w:2 * fox_w], pf[:, 2 * fox_w:]
    c0 += 3 * fox_w
    pff = jnp.dot(h, w_ref[:, c0:c0 + LANES], preferred_element_type=F32)
    lff = _log_sigmoid(pff + fb_ref[...])
    if sample:
        fq_ref[...] = (fq * (dh ** -0.5)).reshape(fq_ref.shape)
        fk_ref[...] = fk.reshape(fk_ref.shape)
        fv_ref[...] = fv.reshape(fv_ref.shape)
        for s in range(n_steps):
            rows = slice(s * nb, (s + 1) * nb)
            fkt_ref[s] = fk[rows].T
            fvt_ref[s] = fv[rows].T
            lfft_ref[s] = lff[rows].T[:nh]
        return

    fvt = fv.T
    fqt = (fq * (LOG2E * dh ** -0.5)).T
    fkt_ref[...] = fk.T
    fvt_ref[...] = fvt
    lfft_ref[...] = lff.T[:nh]
    tri = (_iota((tm, tm), 0) >= _iota((tm, tm), 1)).astype(BF16)
    c_nat = _dot_exact_lhs(tri, _split3(lff))
    c_rel = c_nat * LOG2E
    c_rel_t = c_rel.T
    base = ccarry_ref[...]
    cbase_ref[...] = base * LOG2E
    ccarry_ref[...] = base + jnp.broadcast_to(c_nat[tm - 1:tm], ccarry_ref.shape)

    lane = _iota((tm, LANES), 1)
    srow = _iota((SUBLANES, tm), 0)
    npz = N_BIAS_PARTS
    pad_rows = jnp.zeros((LANES - dh - SUBLANES, tm), F32)
    c_parts = [pp.astype(F32) for pp in _split3(c_rel)]
    for hh in range(nh):
        ck = [jnp.broadcast_to(pp[:, hh:hh + 1], (tm, LANES)) for pp in c_parts]
        extra = jnp.where(lane < dh + npz, 1.0, 0.0)
        for j in range(npz):
            extra = jnp.where(lane == dh + npz + j, -ck[j], extra)
        ka_ref[:, hh * LANES:(hh + 1) * LANES] = jnp.where(lane < dh, _head_rows(fk, hh, dh), extra).astype(BF16)
        cq = [pp.astype(F32) for pp in _split3(c_rel_t[hh:hh + 1, :])]
        grp = jnp.where((srow >= npz) & (srow < 2 * npz), 1.0, 0.0)
        for j in range(npz):
            grp = jnp.where(srow == j, cq[j], grp)
        qat_ref[hh * LANES:(hh + 1) * LANES, :] = jnp.concatenate(
            [fqt[hh * dh:(hh + 1) * dh], grp, pad_rows], axis=0).astype(BF16)
        vat_ref[hh * LANES:(hh + 1) * LANES, :] = jnp.concatenate(
            [fvt[hh * dh:(hh + 1) * dh], jnp.where(srow == 0, 1.0, 0.0), pad_rows], axis=0).astype(BF16)


def _inproj(x, mod, ln_in, w_in_p, conv_w, lb_logits, fbias, prev, *, layer, n_seq, sample):
    t, d = x.shape
    tm = TOKEN_TILE
    n_tiles = t // tm
    seq = t // n_seq
    tiles_per_seq = seq // tm if not sample else 1
    n_steps = t // n_seq if sample else 0
    convw = conv_w.shape[1]
    hgw = lb_logits.shape[1]
    ncols = w_in_p.shape[1]
    foxw = (ncols - LANES - 3 * convw - 4 * hgw) // 3
    nh = foxw // HEAD_DIM
    first_layer = ln_in is not None
    depth = mod.shape[0]

    inputs, in_specs = [x], [pl.BlockSpec((tm, d), lambda i: (i, 0))]
    if sample:
        assert n_tiles == 1
        inputs.append(mod)
        in_specs.append(pl.BlockSpec((None, n_seq, 2 * d), lambda i: (layer, 0, 0)))
    else:
        rows = mod.shape[1]
        inputs.append(mod.reshape(depth, rows, 1, mod.shape[2]))
        in_specs.append(pl.BlockSpec((None, None, 1, 2 * d),
                                     lambda i: (layer, rows - SUBLANES + i // tiles_per_seq, 0, 0)))
    if first_layer:
        inputs += [ln_in[0].reshape(1, d), ln_in[1].reshape(1, d)]
        in_specs += [pl.BlockSpec((1, d), lambda i: (0, 0))] * 2
    inputs += [w_in_p, conv_w, lb_logits, fbias]
    in_specs += [
        pl.BlockSpec((d, ncols), lambda i: (0, 0), pipeline_mode=pl.Buffered(1)),
        pl.BlockSpec(conv_w.shape, lambda i: (0, 0)),
        pl.BlockSpec(lb_logits.shape, lambda i: (0, 0)),
        pl.BlockSpec((1, LANES), lambda i: (0, 0)),
    ]
    tok = lambda w, dt: (jax.ShapeDtypeStruct((t, w), dt), pl.BlockSpec((tm, w), lambda i: (i, 0)))
    whole = lambda shape, dt: (jax.ShapeDtypeStruct(shape, dt), pl.BlockSpec(shape, lambda i: (0,) * len(shape)))
    scratch = []
    if sample:
        inputs.append(prev)
        in_specs.append(pl.BlockSpec(prev.shape, lambda i: (0, 0)))
        outs = ([tok(convw, BF16), tok(convw, F32)] + [whole((n_steps, hgw, n_seq), F32)] * 4 + [tok(hgw, F32)]
                + [whole((n_steps, n_seq, foxw), F32)] * 3 + [whole((n_steps, foxw, n_seq), F32)] * 2
                + [whole((n_steps, nh, n_seq), F32)])
    else:
        per_seq_t = lambda rows, dt: (
            jax.ShapeDtypeStruct((n_seq, rows, seq), dt),
            pl.BlockSpec((None, rows, tm), lambda i: (i // tiles_per_seq, 0, i % tiles_per_seq)))
        outs = ([tok(convw, BF16),
                 (jax.ShapeDtypeStruct((n_seq, SUBLANES, convw), F32),
                  pl.BlockSpec((None, SUBLANES, convw), lambda i: (i // tiles_per_seq, 0, 0)))]
                + [tok(hgw, F32)] * 5
                + [per_seq_t(nh * LANES, BF16), tok(nh * LANES, BF16), per_seq_t(nh * LANES, BF16),
                   per_seq_t(foxw, F32), per_seq_t(foxw, F32), per_seq_t(nh, F32),
                   (jax.ShapeDtypeStruct((n_tiles, SUBLANES, LANES), F32),
                    pl.BlockSpec((None, SUBLANES, LANES), lambda i: (i, 0, 0)))])
        scratch = [pltpu.VMEM((SUBLANES, convw), F32), pltpu.VMEM((SUBLANES, LANES), F32)]
    kern = functools.partial(_inproj_kernel, layer=layer, sample=sample, first_layer=first_layer,
                             tiles_per_seq=tiles_per_seq, n_steps=n_steps, d_model=d, conv_w=convw,
                             hgrn_w=hgw, fox_w=foxw)
    return pl.pallas_call(
        kern,
        grid=(n_tiles,),
        in_specs=in_specs,
        out_specs=[o[1] for o in outs],
        out_shape=[o[0] for o in outs],
        scratch_shapes=scratch,
        compiler_params=_params(("arbitrary",)),
        name="inproj_sample" if sample else "inproj_prompt",
    )(*inputs)


def _hgrn_prompt_kernel(q_ref, k_ref, g_ref, v_ref, o_ref, st_out_ref, st_ref, *, chunk, n_heads):
    i = pl.program_id(1)
    tm, w = q_ref.shape
    dh = w // n_heads
    c = chunk

    @pl.when(i == 0)
    def _():
        st_ref[...] = jnp.zeros_like(st_ref)

    r_cc, c_cc = _iota((c, c), 0), _iota((c, c), 1)
    tri = (r_cc >= c_cc).astype(BF16)
    lane = _iota((1, w), 1)
    head_masks = [_div(lane, dh) == hh for hh in range(n_heads)]
    bd = _div(_iota((w, w), 0), dh) == _div(_iota((w, w), 1), dh)
    bd_bf = bd.astype(BF16)
    row = _iota((c, 1), 0)
    col_key = _mod(_iota((c, n_heads * c), 1), c)
    row_q = _iota((c, n_heads * c), 0)
    levels = []
    b = c // 2
    while b >= 1:
        levels.append(dict(
            half=b,
            upper=_mod(row, 2 * b) >= b,
            lower=_mod(row, 2 * b) < b,
            same=_div(row_q, 2 * b) == _div(col_key, 2 * b),
        ))
        b //= 2
    sub = _iota((1, SUBLANES, 1), 1)

    def boundary_rows(x, b):
        if 2 * b >= SUBLANES:
            return jnp.concatenate(
                [jnp.broadcast_to(x[s0 + b - 1:s0 + b], (2 * b, w)) for s0 in range(0, c, 2 * b)], axis=0)
        x3 = x.reshape(c // SUBLANES, SUBLANES, w)
        out = None
        for s0 in range(SUBLANES - 2 * b, -1, -2 * b):
            rep = jnp.broadcast_to(x3[:, s0 + b - 1:s0 + b, :], x3.shape)
            out = rep if out is None else jnp.where(sub < s0 + 2 * b, rep, out)
        return out.reshape(c, w)

    chunks = range(tm // c)
    rows = [slice(ci * c, (ci + 1) * c) for ci in chunks]
    q = [q_ref[r, :] for r in rows]
    k = [k_ref[r, :] for r in rows]
    v = [v_ref[r, :] for r in rows]
    vb = [x.astype(BF16) for x in v]
    gcum = [_dot_exact_lhs(tri, _split3(g_ref[r, :])) * LOG2E for r in rows]
    a = [jnp.zeros((c, n_heads * c), F32) for _ in chunks]
    for lv in levels:
        for ci in chunks:
            d = gcum[ci] - boundary_rows(gcum[ci], lv["half"])
            e = jnp.exp2(jnp.where(lv["upper"], d, -d))
            qt = jnp.where(lv["upper"], q[ci] * e, 0.0).astype(BF16)
            kt = jnp.where(lv["lower"], k[ci] * e, 0.0).astype(BF16)
            kbd = jnp.concatenate([jnp.where(hm, kt, jnp.zeros_like(kt)) for hm in head_masks], axis=0)
            xl = lax.dot_general(qt, kbd, NT_DIMS, preferred_element_type=F32)
            a[ci] = jnp.where(lv["same"], xl, a[ci])
    o = []
    for ci in chunks:
        diag = _dot_exact_rhs(_split2(q[ci] * k[ci]), bd_bf)
        vbd = jnp.concatenate([jnp.where(hm, vb[ci], jnp.zeros_like(vb[ci])) for hm in head_masks], axis=0)
        o.append(diag * v[ci] + jnp.dot(a[ci].astype(BF16), vbd, preferred_element_type=F32))
    st = st_ref[...]
    for ci in chunks:
        qg = (q[ci] * jnp.exp2(gcum[ci])).astype(BF16)
        o_ref[rows[ci], :] = o[ci] + lax.dot_general(qg, st.astype(BF16), NT_DIMS, preferred_element_type=F32)
        g_last = gcum[ci][c - 1:c]
        kd = (k[ci] * jnp.exp2(g_last - gcum[ci])).astype(BF16)
        upd = lax.dot_general(vb[ci], kd, TN_DIMS, preferred_element_type=F32)
        st = st * jnp.exp2(g_last) + jnp.where(bd, upd, 0.0)
    st_ref[...] = st

    @pl.when(i == pl.num_programs(1) - 1)
    def _():
        st_out_ref[...] = st_ref[...].T


def _hgrn_prompt(qh, kh, lfh, vh, *, n_seq):
    t, w = qh.shape
    tm = HGRN_TILE
    tps = (t // n_seq) // tm
    spec = pl.BlockSpec((tm, w), lambda b, i: (b * tps + i, 0))
    kern = functools.partial(_hgrn_prompt_kernel, chunk=HGRN_CHUNK, n_heads=HGRN_HEADS)
    return pl.pallas_call(
        kern,
        grid=(n_seq, tps),
        in_specs=[spec] * 4,
        out_specs=[spec, pl.BlockSpec((None, w, w), lambda b, i: (b, 0, 0))],
        out_shape=[jax.ShapeDtypeStruct((t, w), F32), jax.ShapeDtypeStruct((n_seq, w, w), F32)],
        scratch_shapes=[pltpu.VMEM((w, w), F32)],
        compiler_params=_params(("arbitrary", "arbitrary")),
        name="hgrn_prompt",
    )(qh, kh, lfh, vh)


def _hgrn_sample_kernel(q_ref, k_ref, g_ref, v_ref, s0_ref, o_ref, s_out_ref):
    n_steps, dh, nb = q_ref.shape

    def body(d, o_acc):
        s = s0_ref[d]
        out = []
        for t in range(n_steps):
            row = pl.ds(d, 1)
            s = jnp.exp(g_ref[t, row, :]) * s + k_ref[t, row, :] * v_ref[t]
            out.append(o_acc[t] + q_ref[t, row, :] * s)
        s_out_ref[d] = s
        return tuple(out)

    o = lax.fori_loop(0, dh, body, tuple(jnp.zeros((v_ref.shape[1], nb), F32) for _ in range(n_steps)))
    for t in range(n_steps):
        o_ref[t] = o[t]


def _hgrn_sample(qh_t, kh_t, lfh_t, vh_t, state_t, *, layer):
    n_steps, w, nb = qh_t.shape
    _, n_heads, dk, dv, _ = state_t.shape
    gate = pl.BlockSpec((n_steps, dk, nb), lambda h: (0, h, 0))
    val = pl.BlockSpec((n_steps, dv, nb), lambda h: (0, h, 0))
    return pl.pallas_call(
        _hgrn_sample_kernel,
        grid=(n_heads,),
        in_specs=[gate, gate, gate, val,
                  pl.BlockSpec((None, None, dk, dv, nb), lambda h: (layer, h, 0, 0, 0))],
        out_specs=[val, pl.BlockSpec((None, dk, dv, nb), lambda h: (h, 0, 0, 0))],
        out_shape=[jax.ShapeDtypeStruct((n_steps, n_heads * dv, nb), F32),
                   jax.ShapeDtypeStruct((n_heads, dk, dv, nb), F32)],
        compiler_params=_params(("arbitrary",)),
        name="hgrn_sample",
    )(qh_t, kh_t, lfh_t, vh_t, state_t)


def _fox_prompt_kernel(qi_ref, kj_ref, last_ref, ka_ref, qat_ref, vat_ref, cbq_ref, cbk_ref, o_ref,
                       m_scr, acc_scr, s_scr, mask_scr):
    pair = pl.program_id(1)
    qi, kj = qi_ref[pair], kj_ref[pair]
    tk, tq = ka_ref.shape[0], qat_ref.shape[1]
    nh = m_scr.shape[0]
    dh = HEAD_DIM

    @pl.when((pl.program_id(0) == 0) & (pair == 0))
    def _():
        mask_scr[...] = jnp.where(_iota((tk, tq), 0) <= _iota((tk, tq), 1), 0.0, NEG_BIG)

    @pl.when(kj == 0)
    def _():
        m_scr[...] = jnp.full_like(m_scr, NEG_BIG)
        acc_scr[...] = jnp.zeros_like(acc_scr)

    delta = cbq_ref[0:1, :] - cbk_ref[0:1, :]

    nbuf = s_scr.shape[0]

    def tile(masked):
        def scores(hh):
            ls = slice(hh * LANES, (hh + 1) * LANES)
            s = jnp.dot(ka_ref[:, ls], qat_ref[ls, :], preferred_element_type=F32)
            if masked:
                s = s + mask_scr[...]
            s_scr[hh % nbuf] = s

        for hh in range(min(nbuf - 1, nh)):
            scores(hh)
        for hh in range(nh):
            if hh + nbuf - 1 < nh:
                scores(hh + nbuf - 1)
            ls = slice(hh * LANES, (hh + 1) * LANES)
            s = s_scr[hh % nbuf]
            d = delta[:, hh:hh + 1]
            m_prev = m_scr[hh]
            m_next = jnp.maximum(m_prev, jnp.max(s, axis=0, keepdims=True) + d)
            p = jnp.exp2(s - (m_next[0:1] - d))
            alpha = jnp.exp2(m_prev - m_next)
            pv = jnp.dot(vat_ref[ls, :], p.astype(BF16), preferred_element_type=F32)
            acc_scr[hh] = acc_scr[hh] * alpha[0:1] + pv
            m_scr[hh] = m_next

    pl.when(kj < qi)(lambda: tile(False))
    pl.when(kj == qi)(lambda: tile(True))

    @pl.when(last_ref[pair] == 1)
    def _():
        per = LANES // dh
        for slab in range(nh // per):
            vals = []
            for j in range(per):
                acc = acc_scr[slab * per + j]
                vals.append(acc[:dh] / acc[dh:dh + 1])
            o_ref[:, slab * LANES:(slab + 1) * LANES] = jnp.concatenate(vals, axis=0).T.astype(BF16)


def _fox_prompt(ka, qat, vat, cbase, *, n_seq, n_heads):
    t, wa = ka.shape
    tq = tk = TOKEN_TILE
    nq = (t // n_seq) // tq
    pairs = [(i, j) for i in range(nq) for j in range(i + 1)]
    qi_tab = jnp.asarray([p[0] for p in pairs], jnp.int32)
    kj_tab = jnp.asarray([p[1] for p in pairs], jnp.int32)
    last_tab = jnp.asarray([int(p[0] == p[1]) for p in pairs], jnp.int32)
    grid_spec = pltpu.PrefetchScalarGridSpec(
        num_scalar_prefetch=3,
        grid=(n_seq, len(pairs)),
        in_specs=[
            pl.BlockSpec((tk, wa), lambda b, p, qi, kj, last: (b * nq + kj[p], 0)),
            pl.BlockSpec((None, wa, tq), lambda b, p, qi, kj, last: (b, 0, qi[p])),
            pl.BlockSpec((None, wa, tk), lambda b, p, qi, kj, last: (b, 0, kj[p])),
            pl.BlockSpec((None, SUBLANES, LANES), lambda b, p, qi, kj, last: (b * nq + qi[p], 0, 0)),
            pl.BlockSpec((None, SUBLANES, LANES), lambda b, p, qi, kj, last: (b * nq + kj[p], 0, 0)),
        ],
        out_specs=pl.BlockSpec((tq, n_heads * HEAD_DIM), lambda b, p, qi, kj, last: (b * nq + qi[p], 0)),
        scratch_shapes=[pltpu.VMEM((n_heads, SUBLANES, tq), F32), pltpu.VMEM((n_heads, LANES, tq), F32),
                        pltpu.VMEM((FOX_SCORE_BUFFERS, tk, tq), F32), pltpu.VMEM((tk, tq), F32)],
    )
    return pl.pallas_call(
        _fox_prompt_kernel,
        grid_spec=grid_spec,
        out_shape=jax.ShapeDtypeStruct((t, n_heads * HEAD_DIM), BF16),
        compiler_params=_params(("arbitrary", "arbitrary")),
        name="fox_prompt",
    )(qi_tab, kj_tab, last_tab, ka, qat, vat, cbase, cbase)


def _fox_sample_kernel(pt_ref, *refs, n_pages, n_steps, n_heads):
    del pt_ref
    kp = refs[:n_pages]
    vp = refs[n_pages:2 * n_pages]
    lp = refs[2 * n_pages:3 * n_pages]
    q_ref, kn_ref, vn_ref, lfn_ref, o_ref, knew_scr, vnew_scr = refs[3 * n_pages:]
    _, dh, page = kp[0].shape
    w = n_heads * dh
    rows = n_steps * n_heads
    b = pl.program_id(0)
    r = pl.ds(_mod(b, SUBLANES), 1)

    @pl.when(b == 0)
    def _():
        knew_scr[...] = jnp.zeros_like(knew_scr)
        vnew_scr[...] = jnp.zeros_like(vnew_scr)

    hmask = _div(_iota((n_heads, w), 1), dh) == _iota((n_heads, w), 0)
    qbd = []
    for s in range(n_steps):
        knew_scr[s:s + 1, :] = kn_ref[s, r, :]
        vnew_scr[s:s + 1, :] = vn_ref[s, r, :]
        qbd.append(jnp.where(hmask, jnp.broadcast_to(q_ref[s, r, :], (n_heads, w)), 0.0))
    qbd = jnp.concatenate(qbd, axis=0).astype(BF16)

    lf_all = jnp.concatenate([lp[j][...] for j in range(n_pages)], axis=0)
    parts = _split3(lf_all)
    after = (_iota((page, page), 0) > _iota((page, page), 1)).astype(BF16)
    ones = jnp.ones((page, page), BF16)
    excl = sum(jnp.dot(pp, after, preferred_element_type=F32) for pp in parts)
    total = sum(jnp.dot(pp, ones, preferred_element_type=F32) for pp in parts)
    scores = [None] * n_pages
    later = jnp.zeros((n_heads, page), F32)
    for j in range(n_pages - 1, -1, -1):
        ps = slice(j * n_heads, (j + 1) * n_heads)
        bias = excl[ps] + later
        later = later + total[ps]
        kt = kp[j][...].reshape(w, page).astype(BF16)
        s = jnp.dot(qbd, kt, preferred_element_type=F32)
        scores[j] = s + jnp.concatenate([bias] * n_steps, axis=0)
    lane_h = _iota((n_heads, page), 1)
    seq_lane = _iota(lfn_ref.shape[1:], 1) == b
    bias_new = jnp.zeros((n_heads, page), F32)
    cum = jnp.zeros((n_heads, 1), F32)
    for s in range(n_steps):
        cum = cum + jnp.sum(jnp.where(seq_lane, lfn_ref[s], 0.0), axis=1, keepdims=True)
        bias_new = jnp.where(lane_h == s, -cum, bias_new)
    s_new = lax.dot_general(qbd, knew_scr[...].astype(BF16), NT_DIMS, preferred_element_type=F32)
    s_new = s_new + jnp.concatenate([bias_new] * n_steps, axis=0)
    key_pos = _iota((rows, page), 1)
    q_step = _div(_iota((rows, page), 0), n_heads)
    scores.append(jnp.where(key_pos <= q_step, s_new, NEG_BIG))

    m = scores[0].max(axis=1, keepdims=True)
    for s in scores[1:]:
        m = jnp.maximum(m, s.max(axis=1, keepdims=True))
    acc = jnp.zeros((rows, w), F32)
    den = jnp.zeros((rows, 1), F32)
    for j, s in enumerate(scores):
        p = jnp.exp(s - m)
        den = den + p.sum(axis=1, keepdims=True)
        pb = p.astype(BF16)
        if j < n_pages:
            vt = vp[j][...].reshape(w, page).astype(BF16)
            acc = acc + lax.dot_general(pb, vt, NT_DIMS, preferred_element_type=F32)
        else:
            acc = acc + jnp.dot(pb, vnew_scr[...].astype(BF16), preferred_element_type=F32)
    out = acc / den
    for s in range(n_steps):
        blk = jnp.where(hmask, out[s * n_heads:(s + 1) * n_heads], 0.0)
        o_ref[s, r, :] = blk.sum(axis=0, keepdims=True)


def _fox_sample(page_table, ck_t, cv_t, clf_t, fq, fk, fv, lff_t, *, layer):
    nb, n_pages = page_table.shape
    _, n_phys, n_heads, dh, page = ck_t.shape
    n_steps = fq.shape[0]
    w = n_heads * dh
    pt_flat = page_table.reshape(-1)

    def page_spec(shape, j):
        nd = len(shape)
        return pl.BlockSpec((None, None) + shape, lambda b, pt: (layer, pt[b * n_pages + j]) + (0,) * nd)

    in_specs = ([page_spec((n_heads, dh, page), j) for j in range(n_pages)] * 2
                + [page_spec((n_heads, page), j) for j in range(n_pages)])
    step_spec = pl.BlockSpec((n_steps, SUBLANES, w), lambda b, pt: (0, b // SUBLANES, 0))
    in_specs += [step_spec] * 3 + [pl.BlockSpec(lff_t.shape, lambda b, pt: (0, 0, 0))]
    kern = functools.partial(_fox_sample_kernel, n_pages=n_pages, n_steps=n_steps, n_heads=n_heads)
    grid_spec = pltpu.PrefetchScalarGridSpec(
        num_scalar_prefetch=1,
        grid=(nb,),
        in_specs=in_specs,
        out_specs=step_spec,
        scratch_shapes=[pltpu.VMEM((page, w), F32), pltpu.VMEM((page, w), F32)],
    )
    return pl.pallas_call(
        kern,
        grid_spec=grid_spec,
        out_shape=jax.ShapeDtypeStruct((n_steps, nb, w), F32),
        compiler_params=_params(("arbitrary",)),
        name="fox_sample",
    )(pt_flat, *([ck_t] * n_pages), *([cv_t] * n_pages), *([clf_t] * n_pages), fq, fk, fv, lff_t)


def _outffn_kernel(*refs, sample, first_layer, d_model, n_heads, alpha):
    it = iter(refs)
    x_ref, yc_ref, oh_ref, hg_ref, yf_ref, ma_ref, mb_ref = (next(it) for _ in range(7))
    if first_layer:
        lng_ref, lnb_ref = next(it), next(it)
    ng_ref, wo_ref, wg_ref, wu_ref, wd_ref, lg_ref, lb_ref, out_ref, acc_ref = (next(it) for _ in range(9))

    tm = x_ref.shape[0]
    x = x_ref[...]
    if first_layer:
        x = _layer_norm(x, lng_ref[...], lnb_ref[...])
    ma, mb = ma_ref[...], mb_ref[...]
    if sample:
        reps = tm // ma.shape[0]
        ma = jnp.concatenate([ma] * reps, axis=0)
        mb = jnp.concatenate([mb] * reps, axis=0)
        o = jnp.concatenate([oh_ref[s].T for s in range(oh_ref.shape[0])], axis=0)
        yf = yf_ref[...].reshape(tm, yf_ref.shape[2]).astype(BF16)
    else:
        o = oh_ref[...]
        yf = yf_ref[...]
    g1, sh2 = ma[:, :d_model], ma[:, d_model:]
    sc2, g2 = mb[:, :d_model], mb[:, d_model:]

    w = o.shape[1]
    dh = w // n_heads
    bd = (_div(_iota((w, w), 0), dh) == _div(_iota((w, w), 1), dh)).astype(BF16)
    ms = _dot_exact_rhs(_split2(o * o), bd) * (1.0 / dh)
    hg = hg_ref[...]
    yh = (o * lax.rsqrt(ms + RMS_EPS) * ng_ref[...] * (hg * jax.nn.sigmoid(hg))).astype(BF16)

    mixed = jnp.concatenate([yc_ref[...], yh, yf], axis=1)
    mix = jnp.dot(mixed, wo_ref[...], preferred_element_type=F32)
    x1 = _layer_norm(alpha * x + (1.0 + g1) * mix, lg_ref[0:1, :], lb_ref[0:1, :])

    h2 = (x1 * (1.0 + sc2) + sh2).astype(BF16)
    acc_ref[...] = jnp.zeros_like(acc_ref)

    def ffn_chunk(ci, carry):
        gate = jnp.dot(h2, wg_ref[ci], preferred_element_type=F32)
        up = jnp.dot(h2, wu_ref[ci], preferred_element_type=F32)
        act = (gate * jax.nn.sigmoid(gate) * up).astype(BF16)
        acc_ref[...] += jnp.dot(act, wd_ref[ci], preferred_element_type=F32)
        return carry

    lax.fori_loop(0, wg_ref.shape[0], ffn_chunk, 0)
    out_ref[...] = _layer_norm(alpha * x1 + (1.0 + g2) * acc_ref[...], lg_ref[1:2, :], lb_ref[1:2, :])


def _outffn(x, yconv, oh, hg, yfox, mod, ln_in, norm_g, w_out_b, wg, wu, wd, ln_g, ln_b,
            *, layer, n_seq, sample, alpha):
    t, d = x.shape
    tm = TOKEN_TILE if sample else OUTFFN_TILE
    first_layer = ln_in is not None
    depth, rows, _ = mod.shape
    tiles_per_seq = (t // n_seq) // tm if not sample else 1
    tok = lambda a: pl.BlockSpec((tm, a.shape[1]), lambda i: (i, 0))
    const = lambda a: pl.BlockSpec(a.shape, lambda i: (0,) * a.ndim, pipeline_mode=pl.Buffered(1))
    small = lambda a: pl.BlockSpec(a.shape, lambda i: (0,) * a.ndim)

    inputs = [x, yconv, oh, hg, yfox]
    if sample:
        in_specs = [tok(x), tok(yconv), small(oh), tok(hg), small(yfox)]
        inputs += [mod, mod]
        in_specs += [pl.BlockSpec((None, n_seq, 2 * d), lambda i: (layer, 0, 1)),
                     pl.BlockSpec((None, n_seq, 2 * d), lambda i: (layer, 0, 2))]
    else:
        in_specs = [tok(a) for a in inputs]
        m4 = mod.reshape(depth, rows, 1, mod.shape[2])
        inputs += [m4, m4]
        row_of = lambda i: rows - SUBLANES + i // tiles_per_seq
        in_specs += [pl.BlockSpec((None, None, 1, 2 * d), lambda i: (layer, row_of(i), 0, 1)),
                     pl.BlockSpec((None, None, 1, 2 * d), lambda i: (layer, row_of(i), 0, 2))]
    if first_layer:
        inputs += [ln_in[0].reshape(1, d), ln_in[1].reshape(1, d)]
        in_specs += [pl.BlockSpec((1, d), lambda i: (0, 0))] * 2
    ng = norm_g.reshape(1, -1)
    inputs += [ng, w_out_b, wg, wu, wd, ln_g, ln_b]
    in_specs += [small(ng), const(w_out_b), const(wg), const(wu), const(wd), small(ln_g), small(ln_b)]
    kern = functools.partial(_outffn_kernel, sample=sample, first_layer=first_layer, d_model=d,
                             n_heads=HGRN_HEADS, alpha=alpha)
    return pl.pallas_call(
        kern,
        grid=(t // tm,),
        in_specs=in_specs,
        out_specs=pl.BlockSpec((tm, d), lambda i: (i, 0)),
        out_shape=jax.ShapeDtypeStruct((t, d), F32),
        scratch_shapes=[pltpu.VMEM((tm, d), F32)],
        compiler_params=_params(("arbitrary",)),
        name="outffn_sample" if sample else "outffn_prompt",
    )(*inputs)


def kernel(x_prompt, x_sample, c_prompt, c_sample, cache_k, cache_v, cache_logf, state_hgrn, state_conv,
           page_table, ln_in_g, ln_in_b, w_mod, b_mod, w_in, conv_w, hgrn_lb_logits, hgrn_norm_g,
           fox_f_bias, w_out, w_ffn_in, w_ffn_out, ln_g, ln_b):
    nbp, seq, d = x_prompt.shape
    nbs, n_steps, _ = x_sample.shape
    depth = w_mod.shape[0]
    alpha = (2 * depth) ** 0.25
    nh, dh = cache_k.shape[3:]
    dff = w_ffn_out.shape[1]
    nchunk = dff // FFN_CHUNK
    hgw = hgrn_lb_logits.shape[1]
    hdk = hgw // HGRN_HEADS

    c_all = jnp.concatenate([c_sample, c_prompt, jnp.zeros((SUBLANES - nbp, d), F32)], axis=0)
    mod = _modulation(c_all, w_mod, b_mod)

    ln_in = (ln_in_g, ln_in_b)
    xp = x_prompt.reshape(nbp * seq, d)
    xs = x_sample.transpose(1, 0, 2).reshape(n_steps * nbs, d)

    ck_t = cache_k.transpose(0, 1, 3, 4, 2)
    cv_t = cache_v.transpose(0, 1, 3, 4, 2)
    clf_t = cache_logf.transpose(0, 1, 3, 2)
    state_t = state_hgrn.transpose(0, 2, 3, 4, 1)

    outs_p = dict(k=[], v=[], lf=[], s=[], conv=[])
    outs_s = dict(k=[], v=[], lf=[], s=[], conv=[])
    for l in range(depth):
        w_in_p = jnp.pad(w_in[l], ((0, 0), (0, LANES - nh))).astype(BF16)
        fbias = jnp.pad(fox_f_bias[l], (0, LANES - nh)).reshape(1, LANES)
        w_out_b = w_out[l].astype(BF16)
        wg = w_ffn_in[l][:, :dff].reshape(d, nchunk, FFN_CHUNK).transpose(1, 0, 2).astype(BF16)
        wu = w_ffn_in[l][:, dff:].reshape(d, nchunk, FFN_CHUNK).transpose(1, 0, 2).astype(BF16)
        wd = w_ffn_out[l].reshape(nchunk, FFN_CHUNK, d).astype(BF16)
        lin = ln_in if l == 0 else None

        (yconv, ulast, qh, kh, lfh, vh, hg, qat, ka, vat, fkt, fvt, lfft, cbase) = _inproj(
            xp, mod, lin, w_in_p, conv_w[l], hgrn_lb_logits, fbias, None, layer=l, n_seq=nbp, sample=False)
        oh, st_t = _hgrn_prompt(qh, kh, lfh, vh, n_seq=nbp)
        yfox = _fox_prompt(ka, qat, vat, cbase, n_seq=nbp, n_heads=nh)
        xp = _outffn(xp, yconv, oh, hg, yfox, mod, lin, hgrn_norm_g[l], w_out_b, wg, wu, wd,
                     ln_g[l], ln_b[l], layer=l, n_seq=nbp, sample=False, alpha=alpha)
        st4 = st_t.reshape(nbp, HGRN_HEADS, hdk, HGRN_HEADS, hdk)
        outs_p["s"].append(jnp.stack([st4[:, hh, :, hh, :] for hh in range(HGRN_HEADS)], axis=1))
        outs_p["k"].append(fkt)
        outs_p["v"].append(fvt)
        outs_p["lf"].append(lfft)
        outs_p["conv"].append(ulast[:, SUBLANES - (CONV_K - 1):, :])

        prev = state_conv[l].transpose(1, 0, 2).reshape((CONV_K - 1) * nbs, -1)
        (yconv, u, qh, kh, lfh, vh, hg, fq, fk, fv, fkt, fvt, lfft) = _inproj(
            xs, mod, lin, w_in_p, conv_w[l], hgrn_lb_logits, fbias, prev, layer=l, n_seq=nbs, sample=True)
        oh, s_fin = _hgrn_sample(qh, kh, lfh, vh, state_t, layer=l)
        yfox = _fox_sample(page_table, ck_t, cv_t, clf_t, fq, fk, fv, lfft, layer=l)
        xs = _outffn(xs, yconv, oh, hg, yfox, mod, lin, hgrn_norm_g[l], w_out_b, wg, wu, wd,
                     ln_g[l], ln_b[l], layer=l, n_seq=nbs, sample=True, alpha=alpha)
        outs_s["s"].append(s_fin)
        outs_s["k"].append(fkt)
        outs_s["v"].append(fvt)
        outs_s["lf"].append(lfft)
        outs_s["conv"].append(u.reshape(n_steps, nbs, -1)[n_steps - (CONV_K - 1):].transpose(1, 0, 2))

    st = lambda xs_: jnp.stack(xs_)
    y_prompt = xp.reshape(nbp, seq, d)
    y_sample = xs.reshape(n_steps, nbs, d).transpose(1, 0, 2)
    kv_p = lambda a: st(a).reshape(depth, nbp, nh, dh, seq).transpose(0, 1, 4, 2, 3)
    kv_s = lambda a: st(a).reshape(depth, n_steps, nh, dh, nbs).transpose(0, 4, 1, 2, 3)
    return (y_prompt, y_sample,
            kv_p(outs_p["k"]), kv_p(outs_p["v"]), st(outs_p["lf"]).transpose(0, 1, 3, 2),
            st(outs_p["s"]), st(outs_p["conv"]),
            kv_s(outs_s["k"]), kv_s(outs_s["v"]), st(outs_s["lf"]).transpose(0, 3, 1, 2),
            st(outs_s["s"]).transpose(0, 4, 1, 2, 3), st(outs_s["conv"]))
```

```python
import functools

import jax
import jax.numpy as jnp
from jax import lax
from jax.experimental import pallas as pl
from jax.experimental.pallas import tpu as pltpu

F32 = jnp.float32
BF16 = jnp.bfloat16

HEAD_DIM = 64
HGRN_HEADS = 4
CONV_K = 3
LN_EPS = 1e-5
RMS_EPS = 1e-6
NEG_BIG = -1e30
LOG2E = 1.4426950408889634

LANES = 128
SUBLANES = 8
VMEM_LIMIT = 56 * 1024 * 1024
TOKEN_TILE = 512
OUTFFN_TILE = 1024
HGRN_CHUNK = 128
HGRN_TILE = 1024
MOD_TILE_N = 1536
FFN_CHUNK = 256
N_BIAS_PARTS = 3
FOX_SCORE_BUFFERS = 3

NT_DIMS = (((1,), (1,)), ((), ()))
TN_DIMS = (((0,), (0,)), ((), ()))


def _params(semantics):
    return pltpu.CompilerParams(dimension_semantics=semantics, vmem_limit_bytes=VMEM_LIMIT)


def _layer_norm(x, g, b):
    mu = jnp.mean(x, axis=-1, keepdims=True)
    xc = x - mu
    var = jnp.mean(xc * xc, axis=-1, keepdims=True)
    return xc * lax.rsqrt(var + LN_EPS) * g + b


def _softplus_tail(x):
    return jnp.log(1.0 + jnp.exp(-jnp.abs(x)))


def _log_sigmoid(x):
    return -(jnp.maximum(-x, 0.0) + _softplus_tail(x))


def _split3(x):
    hi = x.astype(BF16)
    r = x - hi.astype(F32)
    mid = r.astype(BF16)
    lo = (r - mid.astype(F32)).astype(BF16)
    return hi, mid, lo


def _dot_exact_lhs(a, parts):
    n = parts[0].shape[1]
    r = jnp.dot(a, jnp.concatenate(parts, axis=1), preferred_element_type=F32)
    return sum(r[:, j * n:(j + 1) * n] for j in range(len(parts)))


def _dot_exact_rhs(parts, a):
    m = parts[0].shape[0]
    r = jnp.dot(jnp.concatenate(parts, axis=0), a, preferred_element_type=F32)
    return sum(r[j * m:(j + 1) * m] for j in range(len(parts)))


def _split2(x):
    hi = x.astype(BF16)
    return hi, (x - hi.astype(F32)).astype(BF16)


def _iota(shape, axis):
    return lax.broadcasted_iota(jnp.int32, shape, axis)


def _log2(n):
    assert n > 0 and n & (n - 1) == 0, n
    return n.bit_length() - 1


def _div(x, n):
    return lax.shift_right_logical(x, _log2(n))


def _mod(x, n):
    return jnp.bitwise_and(x, n - 1)


def _mod_kernel(c_ref, w_ref, b_ref, o_ref):
    c = c_ref[...]
    a = (c * jax.nn.sigmoid(c)).astype(BF16)
    o_ref[...] = jnp.dot(a, w_ref[...].astype(BF16), preferred_element_type=F32) + b_ref[...]


def _modulation(c_all, w_mod, b_mod):
    depth, d, n = w_mod.shape
    rows = c_all.shape[0]
    return pl.pallas_call(
        _mod_kernel,
        grid=(depth, n // MOD_TILE_N),
        in_specs=[
            pl.BlockSpec((rows, d), lambda l, j: (0, 0)),
            pl.BlockSpec((None, d, MOD_TILE_N), lambda l, j: (l, 0, j)),
            pl.BlockSpec((None, 1, MOD_TILE_N), lambda l, j: (l, 0, j)),
        ],
        out_specs=pl.BlockSpec((None, rows, MOD_TILE_N), lambda l, j: (l, 0, j)),
        out_shape=jax.ShapeDtypeStruct((depth, rows, n), F32),
        compiler_params=_params(("arbitrary", "arbitrary")),
        name="modulation",
    )(c_all, w_mod, b_mod.reshape(depth, 1, n))


def _head_rows(x, hh, dh):
    per = LANES // dh
    slab = x[:, (hh // per) * LANES:(hh // per + 1) * LANES]
    shift = (hh % per) * dh
    return pltpu.roll(slab, LANES - shift, 1) if shift else slab


def _inproj_kernel(*refs, layer, sample, first_layer, tiles_per_seq, n_steps, d_model, conv_w, hgrn_w, fox_w):
    it = iter(refs)
    x_ref, mod_ref = next(it), next(it)
    if first_layer:
        lng_ref, lnb_ref = next(it), next(it)
    w_ref, cw_ref, lbl_ref, fb_ref = next(it), next(it), next(it), next(it)
    if sample:
        prev_ref = next(it)
        (yconv_ref, u_ref, qh_ref, kh_ref, lfh_ref, vh_ref, hg_ref,
         fq_ref, fk_ref, fv_ref, fkt_ref, fvt_ref, lfft_ref) = (next(it) for _ in range(13))
    else:
        (yconv_ref, ulast_ref, qh_ref, kh_ref, lfh_ref, vh_ref, hg_ref,
         qat_ref, ka_ref, vat_ref, fkt_ref, fvt_ref, lfft_ref, cbase_ref,
         carry_ref, ccarry_ref) = (next(it) for _ in range(16))

    tm = x_ref.shape[0]
    dh = HEAD_DIM
    nh = fox_w // dh
    x = x_ref[...]
    if first_layer:
        x = _layer_norm(x, lng_ref[...], lnb_ref[...])
    m = mod_ref[...]
    if sample:
        m = jnp.concatenate([m] * (tm // m.shape[0]), axis=0)
    sh1, sc1 = m[:, :d_model], m[:, d_model:]
    h = (x * (1.0 + sc1) + sh1).astype(BF16)

    c0 = 0
    pc = jnp.dot(h, w_ref[:, c0:c0 + 3 * conv_w], preferred_element_type=F32)
    cb, cc, ch = pc[:, :conv_w], pc[:, conv_w:2 * conv_w], pc[:, 2 * conv_w:]
    u = cc * ch
    if sample:
        nb = tm // n_steps
        prev0, prev1 = prev_ref[:nb, :], prev_ref[nb:, :]
        u1 = jnp.concatenate([prev1, u[:tm - nb]], axis=0)
        u2 = jnp.concatenate([prev0, prev1, u[:tm - 2 * nb]], axis=0)
        u_ref[...] = u
    else:
        i = pl.program_id(0)

        @pl.when(i % tiles_per_seq == 0)
        def _():
            carry_ref[...] = jnp.zeros_like(carry_ref)
            ccarry_ref[...] = jnp.zeros_like(ccarry_ref)

        prev = carry_ref[...]
        row = _iota((tm, 1), 0)
        last1, last2 = prev[SUBLANES - 1:SUBLANES], prev[SUBLANES - 2:SUBLANES - 1]
        u1 = jnp.where(row == 0, last1, pltpu.roll(u, 1, 0))
        u2 = jnp.where(row == 0, last2, jnp.where(row == 1, last1, pltpu.roll(u, 2, 0)))
        carry_ref[...] = u[tm - SUBLANES:]
        ulast_ref[...] = u[tm - SUBLANES:]
    cw = cw_ref[...]
    yconv_ref[...] = (cb * (cw[0:1] * u2 + cw[1:2] * u1 + cw[2:3] * u)).astype(BF16)

    c0 += 3 * conv_w
    ph = jnp.dot(h, w_ref[:, c0:c0 + 4 * hgrn_w], preferred_element_type=F32)
    hq, hf = ph[:, :hgrn_w], ph[:, hgrn_w:2 * hgrn_w]
    hi, hg = ph[:, 2 * hgrn_w:3 * hgrn_w], ph[:, 3 * hgrn_w:]
    logits = lbl_ref[...]
    e = jnp.exp(logits - jnp.max(logits, axis=0, keepdims=True))
    p = e / jnp.sum(e, axis=0, keepdims=True)
    csum = p[0:1]
    for j in range(1, layer + 1):
        csum = csum + p[j:j + 1]
    lb = csum - p[layer:layer + 1]
    has_lb = lb > 0
    log_lb = jnp.where(has_lb, jnp.log(jnp.where(has_lb, lb, 1.0)), NEG_BIG)
    b2 = jnp.log1p(-lb) + _log_sigmoid(hf)
    lfh = jnp.maximum(log_lb, b2) + _softplus_tail(log_lb - b2)
    kh = (1.0 - lb) * jax.nn.sigmoid(-hf)
    qh = hq * jax.nn.sigmoid(hq)
    hg_ref[...] = hg
    if sample:
        for s in range(n_steps):
            rows = slice(s * nb, (s + 1) * nb)
            qh_ref[s] = qh[rows].T
            kh_ref[s] = kh[rows].T
            lfh_ref[s] = lfh[rows].T
            vh_ref[s] = hi[rows].T
    else:
        qh_ref[...] = qh
        kh_ref[...] = kh
        lfh_ref[...] = lfh
        vh_ref[...] = hi

    c0 += 4 * hgrn_w
    pf = jnp.dot(h, w_ref[:, c0:c0 + 3 * fox_w], preferred_element_type=F32)
    fq, fk, fv = pf[:, :fox_w], pf[:, fox_w:2 * fox_w], pf[:, 2 * fox_w:]
    c0 += 3 * fox_w
    pff = jnp.dot(h, w_ref[:, c0:c0 + LANES], preferred_element_type=F32)
    lff = _log_sigmoid(pff + fb_ref[...])
    if sample:
        fq_ref[...] = (fq * (dh ** -0.5)).reshape(fq_ref.shape)
        fk_ref[...] = fk.reshape(fk_ref.shape)
        fv_ref[...] = fv.reshape(fv_ref.shape)
        for s in range(n_steps):
            rows = slice(s * nb, (s + 1) * nb)
            fkt_ref[s] = fk[rows].T
            fvt_ref[s] = fv[rows].T
            lfft_ref[s] = lff[rows].T[:nh]
        return

    fvt = fv.T
    fqt = (fq * (LOG2E * dh ** -0.5)).T
    fkt_ref[...] = fk.T
    fvt_ref[...] = fvt
    lfft_ref[...] = lff.T[:nh]
    tri = (_iota((tm, tm), 0) >= _iota((tm, tm), 1)).astype(BF16)
    c_nat = _dot_exact_lhs(tri, _split3(lff))
    c_rel = c_nat * LOG2E
    c_rel_t = c_rel.T
    base = ccarry_ref[...]
    cbase_ref[...] = base * LOG2E
    ccarry_ref[...] = base + jnp.broadcast_to(c_nat[tm - 1:tm], ccarry_ref.shape)

    lane = _iota((tm, LANES), 1)
    srow = _iota((SUBLANES, tm), 0)
    npz = N_BIAS_PARTS
    pad_rows = jnp.zeros((LANES - dh - SUBLANES, tm), F32)
    c_parts = [pp.astype(F32) for pp in _split3(c_rel)]
    for hh in range(nh):
        ck = [jnp.broadcast_to(pp[:, hh:hh + 1], (tm, LANES)) for pp in c_parts]
        extra = jnp.where(lane < dh + npz, 1.0, 0.0)
        for j in range(npz):
            extra = jnp.where(lane == dh + npz + j, -ck[j], extra)
        ka_ref[:, hh * LANES:(hh + 1) * LANES] = jnp.where(lane < dh, _head_rows(fk, hh, dh), extra).astype(BF16)
        cq = [pp.astype(F32) for pp in _split3(c_rel_t[hh:hh + 1, :])]
        grp = jnp.where((srow >= npz) & (srow < 2 * npz), 1.0, 0.0)
        for j in range(npz):
            grp = jnp.where(srow == j, cq[j], grp)
        qat_ref[hh * LANES:(hh + 1) * LANES, :] = jnp.concatenate(
            [fqt[hh * dh:(hh + 1) * dh], grp, pad_rows], axis=0).astype(BF16)
        vat_ref[hh * LANES:(hh + 1) * LANES, :] = jnp.concatenate(
            [fvt[hh * dh:(hh + 1) * dh], jnp.where(srow == 0, 1.0, 0.0), pad_rows], axis=0).astype(BF16)


def _inproj(x, mod, ln_in, w_in_p, conv_w, lb_logits, fbias, prev, *, layer, n_seq, sample):
    t, d = x.shape
    tm = TOKEN_TILE
    n_tiles = t // tm
    seq = t // n_seq
    tiles_per_seq = seq // tm if not sample else 1
    n_steps = t // n_seq if sample else 0
    convw = conv_w.shape[1]
    hgw = lb_logits.shape[1]
    ncols = w_in_p.shape[1]
    foxw = (ncols - LANES - 3 * convw - 4 * hgw) // 3
    nh = foxw // HEAD_DIM
    first_layer = ln_in is not None
    depth = mod.shape[0]

    inputs, in_specs = [x], [pl.BlockSpec((tm, d), lambda i: (i, 0))]
    if sample:
        assert n_tiles == 1
        inputs.append(mod)
        in_specs.append(pl.BlockSpec((None, n_seq, 2 * d), lambda i: (layer, 0, 0)))
    else:
        rows = mod.shape[1]
        inputs.append(mod.reshape(depth, rows, 1, mod.shape[2]))
        in_specs.append(pl.BlockSpec((None, None, 1, 2 * d),
                                     lambda i: (layer, rows - SUBLANES + i // tiles_per_seq, 0, 0)))
    if first_layer:
        inputs += [ln_in[0].reshape(1, d), ln_in[1].reshape(1, d)]
        in_specs += [pl.BlockSpec((1, d), lambda i: (0, 0))] * 2
    inputs += [w_in_p, conv_w, lb_logits, fbias]
    in_specs += [
        pl.BlockSpec((d, ncols), lambda i: (0, 0), pipeline_mode=pl.Buffered(1)),
        pl.BlockSpec(conv_w.shape, lambda i: (0, 0)),
        pl.BlockSpec(lb_logits.shape, lambda i: (0, 0)),
        pl.BlockSpec((1, LANES), lambda i: (0, 0)),
    ]
    tok = lambda w, dt: (jax.ShapeDtypeStruct((t, w), dt), pl.BlockSpec((tm, w), lambda i: (i, 0)))
    whole = lambda shape, dt: (jax.ShapeDtypeStruct(shape, dt), pl.BlockSpec(shape, lambda i: (0,) * len(shape)))
    scratch = []
    if sample:
        inputs.append(prev)
        in_specs.append(pl.BlockSpec(prev.shape, lambda i: (0, 0)))
        outs = ([tok(convw, BF16), tok(convw, F32)] + [whole((n_steps, hgw, n_seq), F32)] * 4 + [tok(hgw, F32)]
                + [whole((n_steps, n_seq, foxw), F32)] * 3 + [whole((n_steps, foxw, n_seq), F32)] * 2
                + [whole((n_steps, nh, n_seq), F32)])
    else:
        per_seq_t = lambda rows, dt: (
            jax.ShapeDtypeStruct((n_seq, rows, seq), dt),
            pl.BlockSpec((None, rows, tm), lambda i: (i // tiles_per_seq, 0, i % tiles_per_seq)))
        outs = ([tok(convw, BF16),
                 (jax.ShapeDtypeStruct((n_seq, SUBLANES, convw), F32),
                  pl.BlockSpec((None, SUBLANES, convw), lambda i: (i // tiles_per_seq, 0, 0)))]
                + [tok(hgw, F32)] * 5
                + [per_seq_t(nh * LANES, BF16), tok(nh * LANES, BF16), per_seq_t(nh * LANES, BF16),
                   per_seq_t(foxw, F32), per_seq_t(foxw, F32), per_seq_t(nh, F32),
                   (jax.ShapeDtypeStruct((n_tiles, SUBLANES, LANES), F32),
                    pl.BlockSpec((None, SUBLANES, LANES), lambda i: (i, 0, 0)))])
        scratch = [pltpu.VMEM((SUBLANES, convw), F32), pltpu.VMEM((SUBLANES, LANES), F32)]
    kern = functools.partial(_inproj_kernel, layer=layer, sample=sample, first_layer=first_layer,
                             tiles_per_seq=tiles_per_seq, n_steps=n_steps, d_model=d, conv_w=convw,
                             hgrn_w=hgw, fox_w=foxw)
    return pl.pallas_call(
        kern,
        grid=(n_tiles,),
        in_specs=in_specs,
        out_specs=[o[1] for o in outs],
        out_shape=[o[0] for o in outs],
        scratch_shapes=scratch,
        compiler_params=_params(("arbitrary",)),
        name="inproj_sample" if sample else "inproj_prompt",
    )(*inputs)


def _hgrn_prompt_kernel(q_ref, k_ref, g_ref, v_ref, o_ref, st_out_ref, st_ref, *, chunk, n_heads):
    i = pl.program_id(1)
    tm, w = q_ref.shape
    dh = w // n_heads
    c = chunk

    @pl.when(i == 0)
    def _():
        st_ref[...] = jnp.zeros_like(st_ref)

    r_cc, c_cc = _iota((c, c), 0), _iota((c, c), 1)
    tri = (r_cc >= c_cc).astype(BF16)
    lane = _iota((1, w), 1)
    head_masks = [_div(lane, dh) == hh for hh in range(n_heads)]
    bd = _div(_iota((w, w), 0), dh) == _div(_iota((w, w), 1), dh)
    bd_bf = bd.astype(BF16)
    row = _iota((c, 1), 0)
    col_key = _mod(_iota((c, n_heads * c), 1), c)
    row_q = _iota((c, n_heads * c), 0)
    levels = []
    b = c // 2
    while b >= 1:
        levels.append(dict(
            half=b,
            upper=_mod(row, 2 * b) >= b,
            lower=_mod(row, 2 * b) < b,
            same=_div(row_q, 2 * b) == _div(col_key, 2 * b),
        ))
        b //= 2
    sub = _iota((1, SUBLANES, 1), 1)

    def boundary_rows(x, b):
        if 2 * b >= SUBLANES:
            return jnp.concatenate(
                [jnp.broadcast_to(x[s0 + b - 1:s0 + b], (2 * b, w)) for s0 in range(0, c, 2 * b)], axis=0)
        x3 = x.reshape(c // SUBLANES, SUBLANES, w)
        out = None
        for s0 in range(SUBLANES - 2 * b, -1, -2 * b):
            rep = jnp.broadcast_to(x3[:, s0 + b - 1:s0 + b, :], x3.shape)
            out = rep if out is None else jnp.where(sub < s0 + 2 * b, rep, out)
        return out.reshape(c, w)

    chunks = range(tm // c)
    rows = [slice(ci * c, (ci + 1) * c) for ci in chunks]
    q = [q_ref[r, :] for r in rows]
    k = [k_ref[r, :] for r in rows]
    v = [v_ref[r, :] for r in rows]
    vb = [x.astype(BF16) for x in v]
    gcum = [_dot_exact_lhs(tri, _split3(g_ref[r, :])) * LOG2E for r in rows]
    a = [jnp.zeros((c, n_heads * c), F32) for _ in chunks]
    for lv in levels:
        for ci in chunks:
            d = gcum[ci] - boundary_rows(gcum[ci], lv["half"])
            e = jnp.exp2(jnp.where(lv["upper"], d, -d))
            qt = jnp.where(lv["upper"], q[ci] * e, 0.0).astype(BF16)
            kt = jnp.where(lv["lower"], k[ci] * e, 0.0).astype(BF16)
            kbd = jnp.concatenate([jnp.where(hm, kt, jnp.zeros_like(kt)) for hm in head_masks], axis=0)
            xl = lax.dot_general(qt, kbd, NT_DIMS, preferred_element_type=F32)
            a[ci] = jnp.where(lv["same"], xl, a[ci])
    o = []
    for ci in chunks:
        diag = _dot_exact_rhs(_split2(q[ci] * k[ci]), bd_bf)
        vbd = jnp.concatenate([jnp.where(hm, vb[ci], jnp.zeros_like(vb[ci])) for hm in head_masks], axis=0)
        o.append(diag * v[ci] + jnp.dot(a[ci].astype(BF16), vbd, preferred_element_type=F32))
    st = st_ref[...]
    for ci in chunks:
        qg = (q[ci] * jnp.exp2(gcum[ci])).astype(BF16)
        o_ref[rows[ci], :] = o[ci] + lax.dot_general(qg, st.astype(BF16), NT_DIMS, preferred_element_type=F32)
        g_last = gcum[ci][c - 1:c]
        kd = (k[ci] * jnp.exp2(g_last - gcum[ci])).astype(BF16)
        upd = lax.dot_general(vb[ci], kd, TN_DIMS, preferred_element_type=F32)
        st = st * jnp.exp2(g_last) + jnp.where(bd, upd, 0.0)
    st_ref[...] = st

    @pl.when(i == pl.num_programs(1) - 1)
    def _():
        st_out_ref[...] = st_ref[...].T


def _hgrn_prompt(qh, kh, lfh, vh, *, n_seq):
    t, w = qh.shape
    tm = HGRN_TILE
    tps = (t // n_seq) // tm
    spec = pl.BlockSpec((tm, w), lambda b, i: (b * tps + i, 0))
    kern = functools.partial(_hgrn_prompt_kernel, chunk=HGRN_CHUNK, n_heads=HGRN_HEADS)
    return pl.pallas_call(
        kern,
        grid=(n_seq, tps),
        in_specs=[spec] * 4,
        out_specs=[spec, pl.BlockSpec((None, w, w), lambda b, i: (b, 0, 0))],
        out_shape=[jax.ShapeDtypeStruct((t, w), F32), jax.ShapeDtypeStruct((n_seq, w, w), F32)],
        scratch_shapes=[pltpu.VMEM((w, w), F32)],
        compiler_params=_params(("arbitrary", "arbitrary")),
        name="hgrn_prompt",
    )(qh, kh, lfh, vh)


def _hgrn_sample_kernel(q_ref, k_ref, g_ref, v_ref, s0_ref, o_ref, s_out_ref):
    n_steps, dh, nb = q_ref.shape

    def body(d, o_acc):
        s = s0_ref[d]
        out = []
        for t in range(n_steps):
            row = pl.ds(d, 1)
            s = jnp.exp(g_ref[t, row, :]) * s + k_ref[t, row, :] * v_ref[t]
            out.append(o_acc[t] + q_ref[t, row, :] * s)
        s_out_ref[d] = s
        return tuple(out)

    o = lax.fori_loop(0, dh, body, tuple(jnp.zeros((v_ref.shape[1], nb), F32) for _ in range(n_steps)))
    for t in range(n_steps):
        o_ref[t] = o[t]


def _hgrn_sample(qh_t, kh_t, lfh_t, vh_t, state_t, *, layer):
    n_steps, w, nb = qh_t.shape
    _, n_heads, dk, dv, _ = state_t.shape
    gate = pl.BlockSpec((n_steps, dk, nb), lambda h: (0, h, 0))
    val = pl.BlockSpec((n_steps, dv, nb), lambda h: (0, h, 0))
    return pl.pallas_call(
        _hgrn_sample_kernel,
        grid=(n_heads,),
        in_specs=[gate, gate, gate, val,
                  pl.BlockSpec((None, None, dk, dv, nb), lambda h: (layer, h, 0, 0, 0))],
        out_specs=[val, pl.BlockSpec((None, dk, dv, nb), lambda h: (h, 0, 0, 0))],
        out_shape=[jax.ShapeDtypeStruct((n_steps, n_heads * dv, nb), F32),
                   jax.ShapeDtypeStruct((n_heads, dk, dv, nb), F32)],
        compiler_params=_params(("arbitrary",)),
        name="hgrn_sample",
    )(qh_t, kh_t, lfh_t, vh_t, state_t)


def _fox_prompt_kernel(qi_ref, ka_idx_ref, kb_idx_ref, kind_ref, last_ref, ka_a, ka_b, qat_ref, vat_a, vat_b,
                       cbq_ref, cbk_a, cbk_b, o_ref, m_scr, acc_scr, s_scr, mask_scr):
    step = pl.program_id(1)
    kind = kind_ref[step]
    tk, tq = ka_a.shape[0], qat_ref.shape[1]
    nh = m_scr.shape[0]
    dh = HEAD_DIM

    @pl.when((pl.program_id(0) == 0) & (step == 0))
    def _():
        mask_scr[...] = jnp.where(_iota((tk, tq), 0) <= _iota((tk, tq), 1), 0.0, NEG_BIG)

    @pl.when(ka_idx_ref[step] == 0)
    def _():
        m_scr[...] = jnp.full_like(m_scr, NEG_BIG)
        acc_scr[...] = jnp.zeros_like(acc_scr)

    kas, vats = (ka_a, ka_b), (vat_a, vat_b)
    deltas = (cbq_ref[0:1, :] - cbk_a[0:1, :], cbq_ref[0:1, :] - cbk_b[0:1, :])
    nbuf = s_scr.shape[0]

    def tiles(masked):
        items = [(t, hh) for t in range(len(masked)) for hh in range(nh)]

        def scores(n):
            t, hh = items[n]
            ls = slice(hh * LANES, (hh + 1) * LANES)
            s = jnp.dot(kas[t][:, ls], qat_ref[ls, :], preferred_element_type=F32)
            if masked[t]:
                s = s + mask_scr[...]
            s_scr[n % nbuf] = s

        for n in range(min(nbuf - 1, len(items))):
            scores(n)
        for n, (t, hh) in enumerate(items):
            if n + nbuf - 1 < len(items):
                scores(n + nbuf - 1)
            ls = slice(hh * LANES, (hh + 1) * LANES)
            s = s_scr[n % nbuf]
            d = deltas[t][:, hh:hh + 1]
            m_prev = m_scr[hh]
            m_next = jnp.maximum(m_prev, jnp.max(s, axis=0, keepdims=True) + d)
            p = jnp.exp2(s - (m_next[0:1] - d))
            alpha = jnp.exp2(m_prev - m_next)
            pv = jnp.dot(vats[t][ls, :], p.astype(BF16), preferred_element_type=F32)
            acc_scr[hh] = acc_scr[hh] * alpha[0:1] + pv
            m_scr[hh] = m_next

    pl.when(kind == 0)(lambda: tiles((False, False)))
    pl.when(kind == 1)(lambda: tiles((False, True)))
    pl.when(kind == 2)(lambda: tiles((True,)))

    @pl.when(last_ref[step] == 1)
    def _():
        per = LANES // dh
        for slab in range(nh // per):
            vals = []
            for j in range(per):
                acc = acc_scr[slab * per + j]
                vals.append(acc[:dh] / acc[dh:dh + 1])
            o_ref[:, slab * LANES:(slab + 1) * LANES] = jnp.concatenate(vals, axis=0).T.astype(BF16)


def _fox_prompt(ka, qat, vat, cbase, *, n_seq, n_heads):
    t, wa = ka.shape
    tq = tk = TOKEN_TILE
    nq = (t // n_seq) // tq
    steps = []
    for i in range(nq):
        for a in range(0, i + 1, 2):
            two = a + 1 <= i
            steps.append((i, a, a + 1 if two else a, (1 if a + 1 == i else 0) if two else 2,
                          int(a + 1 >= i)))
    tabs = [jnp.asarray([st[c] for st in steps], jnp.int32) for c in range(5)]
    k_rows = lambda tab: pl.BlockSpec((tk, wa), lambda b, p, qi, ka_i, kb_i, kind, last: (b * nq + (ka_i, kb_i)[tab][p], 0))
    k_cols = lambda tab: pl.BlockSpec((None, wa, tk), lambda b, p, qi, ka_i, kb_i, kind, last: (b, 0, (ka_i, kb_i)[tab][p]))
    k_base = lambda tab: pl.BlockSpec((None, SUBLANES, LANES),
                                      lambda b, p, qi, ka_i, kb_i, kind, last: (b * nq + (ka_i, kb_i)[tab][p], 0, 0))
    grid_spec = pltpu.PrefetchScalarGridSpec(
        num_scalar_prefetch=5,
        grid=(n_seq, len(steps)),
        in_specs=[
            k_rows(0), k_rows(1),
            pl.BlockSpec((None, wa, tq), lambda b, p, qi, ka_i, kb_i, kind, last: (b, 0, qi[p])),
            k_cols(0), k_cols(1),
            pl.BlockSpec((None, SUBLANES, LANES), lambda b, p, qi, ka_i, kb_i, kind, last: (b * nq + qi[p], 0, 0)),
            k_base(0), k_base(1),
        ],
        out_specs=pl.BlockSpec((tq, n_heads * HEAD_DIM), lambda b, p, qi, ka_i, kb_i, kind, last: (b * nq + qi[p], 0)),
        scratch_shapes=[pltpu.VMEM((n_heads, SUBLANES, tq), F32), pltpu.VMEM((n_heads, LANES, tq), F32),
                        pltpu.VMEM((FOX_SCORE_BUFFERS, tk, tq), F32), pltpu.VMEM((tk, tq), F32)],
    )
    return pl.pallas_call(
        _fox_prompt_kernel,
        grid_spec=grid_spec,
        out_shape=jax.ShapeDtypeStruct((t, n_heads * HEAD_DIM), BF16),
        compiler_params=_params(("arbitrary", "arbitrary")),
        name="fox_prompt",
    )(*tabs, ka, ka, qat, vat, vat, cbase, cbase, cbase)


def _fox_sample_kernel(pt_ref, *refs, n_pages, n_steps, n_heads):
    del pt_ref
    kp = refs[:n_pages]
    vp = refs[n_pages:2 * n_pages]
    lp = refs[2 * n_pages:3 * n_pages]
    q_ref, kn_ref, vn_ref, lfn_ref, o_ref, knew_scr, vnew_scr = refs[3 * n_pages:]
    _, dh, page = kp[0].shape
    w = n_heads * dh
    rows = n_steps * n_heads
    b = pl.program_id(0)
    r = pl.ds(_mod(b, SUBLANES), 1)

    @pl.when(b == 0)
    def _():
        knew_scr[...] = jnp.zeros_like(knew_scr)
        vnew_scr[...] = jnp.zeros_like(vnew_scr)

    hmask = _div(_iota((n_heads, w), 1), dh) == _iota((n_heads, w), 0)
    qbd = []
    for s in range(n_steps):
        knew_scr[s:s + 1, :] = kn_ref[s, r, :]
        vnew_scr[s:s + 1, :] = vn_ref[s, r, :]
        qbd.append(jnp.where(hmask, jnp.broadcast_to(q_ref[s, r, :], (n_heads, w)), 0.0))
    qbd = jnp.concatenate(qbd, axis=0).astype(BF16)

    lf_all = jnp.concatenate([lp[j][...] for j in range(n_pages)], axis=0)
    parts = _split3(lf_all)
    after = (_iota((page, page), 0) > _iota((page, page), 1)).astype(BF16)
    ones = jnp.ones((page, page), BF16)
    excl = sum(jnp.dot(pp, after, preferred_element_type=F32) for pp in parts)
    total = sum(jnp.dot(pp, ones, preferred_element_type=F32) for pp in parts)
    scores = [None] * n_pages
    later = jnp.zeros((n_heads, page), F32)
    for j in range(n_pages - 1, -1, -1):
        ps = slice(j * n_heads, (j + 1) * n_heads)
        bias = excl[ps] + later
        later = later + total[ps]
        kt = kp[j][...].reshape(w, page).astype(BF16)
        s = jnp.dot(qbd, kt, preferred_element_type=F32)
        scores[j] = s + jnp.concatenate([bias] * n_steps, axis=0)
    lane_h = _iota((n_heads, page), 1)
    seq_lane = _iota(lfn_ref.shape[1:], 1) == b
    bias_new = jnp.zeros((n_heads, page), F32)
    cum = jnp.zeros((n_heads, 1), F32)
    for s in range(n_steps):
        cum = cum + jnp.sum(jnp.where(seq_lane, lfn_ref[s], 0.0), axis=1, keepdims=True)
        bias_new = jnp.where(lane_h == s, -cum, bias_new)
    s_new = lax.dot_general(qbd, knew_scr[...].astype(BF16), NT_DIMS, preferred_element_type=F32)
    s_new = s_new + jnp.concatenate([bias_new] * n_steps, axis=0)
    key_pos = _iota((rows, page), 1)
    q_step = _div(_iota((rows, page), 0), n_heads)
    scores.append(jnp.where(key_pos <= q_step, s_new, NEG_BIG))

    m = scores[0].max(axis=1, keepdims=True)
    for s in scores[1:]:
        m = jnp.maximum(m, s.max(axis=1, keepdims=True))
    acc = jnp.zeros((rows, w), F32)
    den = jnp.zeros((rows, 1), F32)
    for j, s in enumerate(scores):
        p = jnp.exp(s - m)
        den = den + p.sum(axis=1, keepdims=True)
        pb = p.astype(BF16)
        if j < n_pages:
            vt = vp[j][...].reshape(w, page).astype(BF16)
            acc = acc + lax.dot_general(pb, vt, NT_DIMS, preferred_element_type=F32)
        else:
            acc = acc + jnp.dot(pb, vnew_scr[...].astype(BF16), preferred_element_type=F32)
    out = acc / den
    for s in range(n_steps):
        blk = jnp.where(hmask, out[s * n_heads:(s + 1) * n_heads], 0.0)
        o_ref[s, r, :] = blk.sum(axis=0, keepdims=True)


def _fox_sample(page_table, ck_t, cv_t, clf_t, fq, fk, fv, lff_t, *, layer):
    nb, n_pages = page_table.shape
    _, n_phys, n_heads, dh, page = ck_t.shape
    n_steps = fq.shape[0]
    w = n_heads * dh
    pt_flat = page_table.reshape(-1)

    def page_spec(shape, j):
        nd = len(shape)
        return pl.BlockSpec((None, None) + shape, lambda b, pt: (layer, pt[b * n_pages + j]) + (0,) * nd)

    in_specs = ([page_spec((n_heads, dh, page), j) for j in range(n_pages)] * 2
                + [page_spec((n_heads, page), j) for j in range(n_pages)])
    step_spec = pl.BlockSpec((n_steps, SUBLANES, w), lambda b, pt: (0, b // SUBLANES, 0))
    in_specs += [step_spec] * 3 + [pl.BlockSpec(lff_t.shape, lambda b, pt: (0, 0, 0))]
    kern = functools.partial(_fox_sample_kernel, n_pages=n_pages, n_steps=n_steps, n_heads=n_heads)
    grid_spec = pltpu.PrefetchScalarGridSpec(
        num_scalar_prefetch=1,
        grid=(nb,),
        in_specs=in_specs,
        out_specs=step_spec,
        scratch_shapes=[pltpu.VMEM((page, w), F32), pltpu.VMEM((page, w), F32)],
    )
    return pl.pallas_call(
        kern,
        grid_spec=grid_spec,
        out_shape=jax.ShapeDtypeStruct((n_steps, nb, w), F32),
        compiler_params=_params(("arbitrary",)),
        name="fox_sample",
    )(pt_flat, *([ck_t] * n_pages), *([cv_t] * n_pages), *([clf_t] * n_pages), fq, fk, fv, lff_t)


def _outffn_kernel(*refs, sample, first_layer, d_model, n_heads, alpha):
    it = iter(refs)
    x_ref, yc_ref, oh_ref, hg_ref, yf_ref, ma_ref, mb_ref = (next(it) for _ in range(7))
    if first_layer:
        lng_ref, lnb_ref = next(it), next(it)
    ng_ref, wo_ref, wg_ref, wu_ref, wd_ref, lg_ref, lb_ref, out_ref, acc_ref = (next(it) for _ in range(9))

    tm = x_ref.shape[0]
    x = x_ref[...]
    if first_layer:
        x = _layer_norm(x, lng_ref[...], lnb_ref[...])
    ma, mb = ma_ref[...], mb_ref[...]
    if sample:
        reps = tm // ma.shape[0]
        ma = jnp.concatenate([ma] * reps, axis=0)
        mb = jnp.concatenate([mb] * reps, axis=0)
        o = jnp.concatenate([oh_ref[s].T for s in range(oh_ref.shape[0])], axis=0)
        yf = yf_ref[...].reshape(tm, yf_ref.shape[2]).astype(BF16)
    else:
        o = oh_ref[...]
        yf = yf_ref[...]
    g1, sh2 = ma[:, :d_model], ma[:, d_model:]
    sc2, g2 = mb[:, :d_model], mb[:, d_model:]

    w = o.shape[1]
    dh = w // n_heads
    bd = (_div(_iota((w, w), 0), dh) == _div(_iota((w, w), 1), dh)).astype(BF16)
    ms = _dot_exact_rhs(_split2(o * o), bd) * (1.0 / dh)
    hg = hg_ref[...]
    yh = (o * lax.rsqrt(ms + RMS_EPS) * ng_ref[...] * (hg * jax.nn.sigmoid(hg))).astype(BF16)

    mixed = jnp.concatenate([yc_ref[...], yh, yf], axis=1)
    mix = jnp.dot(mixed, wo_ref[...], preferred_element_type=F32)
    x1 = _layer_norm(alpha * x + (1.0 + g1) * mix, lg_ref[0:1, :], lb_ref[0:1, :])

    h2 = (x1 * (1.0 + sc2) + sh2).astype(BF16)
    acc_ref[...] = jnp.zeros_like(acc_ref)

    def ffn_chunk(ci, carry):
        gate = jnp.dot(h2, wg_ref[ci], preferred_element_type=F32)
        up = jnp.dot(h2, wu_ref[ci], preferred_element_type=F32)
        act = (gate * jax.nn.sigmoid(gate) * up).astype(BF16)
        acc_ref[...] += jnp.dot(act, wd_ref[ci], preferred_element_type=F32)
        return carry

    lax.fori_loop(0, wg_ref.shape[0], ffn_chunk, 0)
    out_ref[...] = _layer_norm(alpha * x1 + (1.0 + g2) * acc_ref[...], lg_ref[1:2, :], lb_ref[1:2, :])


def _outffn(x, yconv, oh, hg, yfox, mod, ln_in, norm_g, w_out_b, wg, wu, wd, ln_g, ln_b,
            *, layer, n_seq, sample, alpha):
    t, d = x.shape
    tm = TOKEN_TILE if sample else OUTFFN_TILE
    first_layer = ln_in is not None
    depth, rows, _ = mod.shape
    tiles_per_seq = (t // n_seq) // tm if not sample else 1
    tok = lambda a: pl.BlockSpec((tm, a.shape[1]), lambda i: (i, 0))
    const = lambda a: pl.BlockSpec(a.shape, lambda i: (0,) * a.ndim, pipeline_mode=pl.Buffered(1))
    small = lambda a: pl.BlockSpec(a.shape, lambda i: (0,) * a.ndim)

    inputs = [x, yconv, oh, hg, yfox]
    if sample:
        in_specs = [tok(x), tok(yconv), small(oh), tok(hg), small(yfox)]
        inputs += [mod, mod]
        in_specs += [pl.BlockSpec((None, n_seq, 2 * d), lambda i: (layer, 0, 1)),
                     pl.BlockSpec((None, n_seq, 2 * d), lambda i: (layer, 0, 2))]
    else:
        in_specs = [tok(a) for a in inputs]
        m4 = mod.reshape(depth, rows, 1, mod.shape[2])
        inputs += [m4, m4]
        row_of = lambda i: rows - SUBLANES + i // tiles_per_seq
        in_specs += [pl.BlockSpec((None, None, 1, 2 * d), lambda i: (layer, row_of(i), 0, 1)),
                     pl.BlockSpec((None, None, 1, 2 * d), lambda i: (layer, row_of(i), 0, 2))]
    if first_layer:
        inputs += [ln_in[0].reshape(1, d), ln_in[1].reshape(1, d)]
        in_specs += [pl.BlockSpec((1, d), lambda i: (0, 0))] * 2
    ng = norm_g.reshape(1, -1)
    inputs += [ng, w_out_b, wg, wu, wd, ln_g, ln_b]
    in_specs += [small(ng), const(w_out_b), const(wg), const(wu), const(wd), small(ln_g), small(ln_b)]
    kern = functools.partial(_outffn_kernel, sample=sample, first_layer=first_layer, d_model=d,
                             n_heads=HGRN_HEADS, alpha=alpha)
    return pl.pallas_call(
        kern,
        grid=(t // tm,),
        in_specs=in_specs,
        out_specs=pl.BlockSpec((tm, d), lambda i: (i, 0)),
        out_shape=jax.ShapeDtypeStruct((t, d), F32),
        scratch_shapes=[pltpu.VMEM((tm, d), F32)],
        compiler_params=_params(("arbitrary",)),
        name="outffn_sample" if sample else "outffn_prompt",
    )(*inputs)


def kernel(x_prompt, x_sample, c_prompt, c_sample, cache_k, cache_v, cache_logf, state_hgrn, state_conv,
           page_table, ln_in_g, ln_in_b, w_mod, b_mod, w_in, conv_w, hgrn_lb_logits, hgrn_norm_g,
           fox_f_bias, w_out, w_ffn_in, w_ffn_out, ln_g, ln_b):
    nbp, seq, d = x_prompt.shape
    nbs, n_steps, _ = x_sample.shape
    depth = w_mod.shape[0]
    alpha = (2 * depth) ** 0.25
    nh, dh = cache_k.shape[3:]
    dff = w_ffn_out.shape[1]
    nchunk = dff // FFN_CHUNK
    hgw = hgrn_lb_logits.shape[1]
    hdk = hgw // HGRN_HEADS

    c_all = jnp.concatenate([c_sample, c_prompt, jnp.zeros((SUBLANES - nbp, d), F32)], axis=0)
    mod = _modulation(c_all, w_mod, b_mod)

    ln_in = (ln_in_g, ln_in_b)
    xp = x_prompt.reshape(nbp * seq, d)
    xs = x_sample.transpose(1, 0, 2).reshape(n_steps * nbs, d)

    ck_t = cache_k.transpose(0, 1, 3, 4, 2)
    cv_t = cache_v.transpose(0, 1, 3, 4, 2)
    clf_t = cache_logf.transpose(0, 1, 3, 2)
    state_t = state_hgrn.transpose(0, 2, 3, 4, 1)

    outs_p = dict(k=[], v=[], lf=[], s=[], conv=[])
    outs_s = dict(k=[], v=[], lf=[], s=[], conv=[])
    for l in range(depth):
        w_in_p = jnp.pad(w_in[l], ((0, 0), (0, LANES - nh))).astype(BF16)
        fbias = jnp.pad(fox_f_bias[l], (0, LANES - nh)).reshape(1, LANES)
        w_out_b = w_out[l].astype(BF16)
        wg = w_ffn_in[l][:, :dff].reshape(d, nchunk, FFN_CHUNK).transpose(1, 0, 2).astype(BF16)
        wu = w_ffn_in[l][:, dff:].reshape(d, nchunk, FFN_CHUNK).transpose(1, 0, 2).astype(BF16)
        wd = w_ffn_out[l].reshape(nchunk, FFN_CHUNK, d).astype(BF16)
        lin = ln_in if l == 0 else None

        (yconv, ulast, qh, kh, lfh, vh, hg, qat, ka, vat, fkt, fvt, lfft, cbase) = _inproj(
            xp, mod, lin, w_in_p, conv_w[l], hgrn_lb_logits, fbias, None, layer=l, n_seq=nbp, sample=False)
        oh, st_t = _hgrn_prompt(qh, kh, lfh, vh, n_seq=nbp)
        yfox = _fox_prompt(ka, qat, vat, cbase, n_seq=nbp, n_heads=nh)
        xp = _outffn(xp, yconv, oh, hg, yfox, mod, lin, hgrn_norm_g[l], w_out_b, wg, wu, wd,
                     ln_g[l], ln_b[l], layer=l, n_seq=nbp, sample=False, alpha=alpha)
        st4 = st_t.reshape(nbp, HGRN_HEADS, hdk, HGRN_HEADS, hdk)
        outs_p["s"].append(jnp.stack([st4[:, hh, :, hh, :] for hh in range(HGRN_HEADS)], axis=1))
        outs_p["k"].append(fkt)
        outs_p["v"].append(fvt)
        outs_p["lf"].append(lfft)
        outs_p["conv"].append(ulast[:, SUBLANES - (CONV_K - 1):, :])

        prev = state_conv[l].transpose(1, 0, 2).reshape((CONV_K - 1) * nbs, -1)
        (yconv, u, qh, kh, lfh, vh, hg, fq, fk, fv, fkt, fvt, lfft) = _inproj(
            xs, mod, lin, w_in_p, conv_w[l], hgrn_lb_logits, fbias, prev, layer=l, n_seq=nbs, sample=True)
        oh, s_fin = _hgrn_sample(qh, kh, lfh, vh, state_t, layer=l)
        yfox = _fox_sample(page_table, ck_t, cv_t, clf_t, fq, fk, fv, lfft, layer=l)
        xs = _outffn(xs, yconv, oh, hg, yfox, mod, lin, hgrn_norm_g[l], w_out_b, wg, wu, wd,
                     ln_g[l], ln_b[l], layer=l, n_seq=nbs, sample=True, alpha=alpha)
        outs_s["s"].append(s_fin)
        outs_s["k"].append(fkt)
        outs_s["v"].append(fvt)
        outs_s["lf"].append(lfft)
        outs_s["conv"].append(u.reshape(n_steps, nbs, -1)[n_steps - (CONV_K - 1):].transpose(1, 0, 2))

    st = lambda xs_: jnp.stack(xs_)
    y_prompt = xp.reshape(nbp, seq, d)
    y_sample = xs.reshape(n_steps, nbs, d).transpose(1, 0, 2)
    kv_p = lambda a: st(a).reshape(depth, nbp, nh, dh, seq).transpose(0, 1, 4, 2, 3)
    kv_s = lambda a: st(a).reshape(depth, n_steps, nh, dh, nbs).transpose(0, 4, 1, 2, 3)
    return (y_prompt, y_sample,
            kv_p(outs_p["k"]), kv_p(outs_p["v"]), st(outs_p["lf"]).transpose(0, 1, 3, 2),
            st(outs_p["s"]), st(outs_p["conv"]),
            kv_s(outs_s["k"]), kv_s(outs_s["v"]), st(outs_s["lf"]).transpose(0, 3, 1, 2),
            st(outs_s["s"]).transpose(0, 4, 1, 2, 3), st(outs_s["conv"]))
```

```python
import functools

import jax
import jax.numpy as jnp
from jax import lax
from jax.experimental import pallas as pl
from jax.experimental.pallas import tpu as pltpu

F32 = jnp.float32
BF16 = jnp.bfloat16

HEAD_DIM = 64
HGRN_HEADS = 4
CONV_K = 3
LN_EPS = 1e-5
RMS_EPS = 1e-6
NEG_BIG = -1e30
LOG2E = 1.4426950408889634

LANES = 128
SUBLANES = 8
VMEM_LIMIT = 56 * 1024 * 1024
TOKEN_TILE = 512
OUTFFN_TILE = 1024
HGRN_CHUNK = 128
HGRN_TILE = 1024
MOD_TILE_N = 1536
FFN_CHUNK = 256
N_BIAS_PARTS = 3
FIRST_ALL_LAYER_OUTPUT = 10
FOX_SCORE_BUFFERS = 3

NT_DIMS = (((1,), (1,)), ((), ()))
TN_DIMS = (((0,), (0,)), ((), ()))


def _params(semantics):
    return pltpu.CompilerParams(dimension_semantics=semantics, vmem_limit_bytes=VMEM_LIMIT)


def _layer_norm(x, g, b):
    mu = jnp.mean(x, axis=-1, keepdims=True)
    xc = x - mu
    var = jnp.mean(xc * xc, axis=-1, keepdims=True)
    return xc * lax.rsqrt(var + LN_EPS) * g + b


def _softplus_tail(x):
    return jnp.log(1.0 + jnp.exp(-jnp.abs(x)))


def _log_sigmoid(x):
    return -(jnp.maximum(-x, 0.0) + _softplus_tail(x))


def _split3(x):
    hi = x.astype(BF16)
    r = x - hi.astype(F32)
    mid = r.astype(BF16)
    lo = (r - mid.astype(F32)).astype(BF16)
    return hi, mid, lo


def _dot_exact_lhs(a, parts):
    n = parts[0].shape[1]
    r = jnp.dot(a, jnp.concatenate(parts, axis=1), preferred_element_type=F32)
    return sum(r[:, j * n:(j + 1) * n] for j in range(len(parts)))


def _dot_exact_rhs(parts, a):
    m = parts[0].shape[0]
    r = jnp.dot(jnp.concatenate(parts, axis=0), a, preferred_element_type=F32)
    return sum(r[j * m:(j + 1) * m] for j in range(len(parts)))


def _split2(x):
    hi = x.astype(BF16)
    return hi, (x - hi.astype(F32)).astype(BF16)


def _iota(shape, axis):
    return lax.broadcasted_iota(jnp.int32, shape, axis)


def _log2(n):
    assert n > 0 and n & (n - 1) == 0, n
    return n.bit_length() - 1


def _div(x, n):
    return lax.shift_right_logical(x, _log2(n))


def _mod(x, n):
    return jnp.bitwise_and(x, n - 1)


def _mod_kernel(c_ref, w_ref, b_ref, o_ref):
    c = c_ref[...]
    a = (c * jax.nn.sigmoid(c)).astype(BF16)
    o_ref[...] = jnp.dot(a, w_ref[...].astype(BF16), preferred_element_type=F32) + b_ref[...]


def _modulation(c_all, w_mod, b_mod):
    depth, d, n = w_mod.shape
    rows = c_all.shape[0]
    return pl.pallas_call(
        _mod_kernel,
        grid=(depth, n // MOD_TILE_N),
        in_specs=[
            pl.BlockSpec((rows, d), lambda l, j: (0, 0)),
            pl.BlockSpec((None, d, MOD_TILE_N), lambda l, j: (l, 0, j)),
            pl.BlockSpec((None, 1, MOD_TILE_N), lambda l, j: (l, 0, j)),
        ],
        out_specs=pl.BlockSpec((None, rows, MOD_TILE_N), lambda l, j: (l, 0, j)),
        out_shape=jax.ShapeDtypeStruct((depth, rows, n), F32),
        compiler_params=_params(("arbitrary", "arbitrary")),
        name="modulation",
    )(c_all, w_mod, b_mod.reshape(depth, 1, n))


def _head_rows(x, hh, dh):
    per = LANES // dh
    slab = x[:, (hh // per) * LANES:(hh // per + 1) * LANES]
    shift = (hh % per) * dh
    return pltpu.roll(slab, LANES - shift, 1) if shift else slab


def _inproj_kernel(*refs, layer, sample, first_layer, tiles_per_seq, n_steps, n_alias, d_model, conv_w, hgrn_w,
                   fox_w):
    it = iter(refs)
    x_ref, mod_ref = next(it), next(it)
    if first_layer:
        lng_ref, lnb_ref = next(it), next(it)
    w_ref, cw_ref, lbl_ref, fb_ref = next(it), next(it), next(it), next(it)
    if sample:
        prev_ref = next(it)
        (yconv_ref, u_ref, qh_ref, kh_ref, lfh_ref, vh_ref, hg_ref,
         fq_ref, fk_ref, fv_ref, fkt_ref, fvt_ref, lfft_ref) = (next(it) for _ in range(13))
    else:
        for _ in range(n_alias):
            next(it)
        (yconv_ref, ulast_ref, qh_ref, kh_ref, lfh_ref, vh_ref, hg_ref,
         qat_ref, ka_ref, vat_ref, fkt_ref, fvt_ref, lfft_ref, cbase_ref,
         carry_ref, ccarry_ref) = (next(it) for _ in range(16))

    tm = x_ref.shape[0]
    dh = HEAD_DIM
    nh = fox_w // dh
    x = x_ref[...]
    if first_layer:
        x = _layer_norm(x, lng_ref[...], lnb_ref[...])
    m = mod_ref[...]
    if sample:
        m = jnp.concatenate([m] * (tm // m.shape[0]), axis=0)
    sh1, sc1 = m[:, :d_model], m[:, d_model:]
    h = (x * (1.0 + sc1) + sh1).astype(BF16)

    c0 = 0
    pc = jnp.dot(h, w_ref[:, c0:c0 + 3 * conv_w], preferred_element_type=F32)
    cb, cc, ch = pc[:, :conv_w], pc[:, conv_w:2 * conv_w], pc[:, 2 * conv_w:]
    u = cc * ch
    if sample:
        nb = tm // n_steps
        prev0, prev1 = prev_ref[:nb, :], prev_ref[nb:, :]
        u1 = jnp.concatenate([prev1, u[:tm - nb]], axis=0)
        u2 = jnp.concatenate([prev0, prev1, u[:tm - 2 * nb]], axis=0)
        u_ref[...] = u
    else:
        i = pl.program_id(0)

        @pl.when(i % tiles_per_seq == 0)
        def _():
            carry_ref[...] = jnp.zeros_like(carry_ref)
            ccarry_ref[...] = jnp.zeros_like(ccarry_ref)

        prev = carry_ref[...]
        row = _iota((tm, 1), 0)
        last1, last2 = prev[SUBLANES - 1:SUBLANES], prev[SUBLANES - 2:SUBLANES - 1]
        u1 = jnp.where(row == 0, last1, pltpu.roll(u, 1, 0))
        u2 = jnp.where(row == 0, last2, jnp.where(row == 1, last1, pltpu.roll(u, 2, 0)))
        carry_ref[...] = u[tm - SUBLANES:]
        ulast_ref[...] = u[tm - SUBLANES:]
    cw = cw_ref[...]
    yconv_ref[...] = (cb * (cw[0:1] * u2 + cw[1:2] * u1 + cw[2:3] * u)).astype(BF16)

    c0 += 3 * conv_w
    ph = jnp.dot(h, w_ref[:, c0:c0 + 4 * hgrn_w], preferred_element_type=F32)
    hq, hf = ph[:, :hgrn_w], ph[:, hgrn_w:2 * hgrn_w]
    hi, hg = ph[:, 2 * hgrn_w:3 * hgrn_w], ph[:, 3 * hgrn_w:]
    logits = lbl_ref[...]
    e = jnp.exp(logits - jnp.max(logits, axis=0, keepdims=True))
    p = e / jnp.sum(e, axis=0, keepdims=True)
    csum = p[0:1]
    for j in range(1, layer + 1):
        csum = csum + p[j:j + 1]
    lb = csum - p[layer:layer + 1]
    has_lb = lb > 0
    log_lb = jnp.where(has_lb, jnp.log(jnp.where(has_lb, lb, 1.0)), NEG_BIG)
    b2 = jnp.log1p(-lb) + _log_sigmoid(hf)
    lfh = jnp.maximum(log_lb, b2) + _softplus_tail(log_lb - b2)
    kh = (1.0 - lb) * jax.nn.sigmoid(-hf)
    qh = hq * jax.nn.sigmoid(hq)
    hg_ref[...] = hg
    if sample:
        for s in range(n_steps):
            rows = slice(s * nb, (s + 1) * nb)
            qh_ref[s] = qh[rows].T
            kh_ref[s] = kh[rows].T
            lfh_ref[s] = lfh[rows].T
            vh_ref[s] = hi[rows].T
    else:
        qh_ref[...] = qh
        kh_ref[...] = kh
        lfh_ref[...] = lfh
        vh_ref[...] = hi

    c0 += 4 * hgrn_w
    pf = jnp.dot(h, w_ref[:, c0:c0 + 3 * fox_w], preferred_element_type=F32)
    fq, fk, fv = pf[:, :fox_w], pf[:, fox_w:2 * fox_w], pf[:, 2 * fox_w:]
    c0 += 3 * fox_w
    pff = jnp.dot(h, w_ref[:, c0:c0 + LANES], preferred_element_type=F32)
    lff = _log_sigmoid(pff + fb_ref[...])
    if sample:
        fq_ref[...] = (fq * (dh ** -0.5)).reshape(fq_ref.shape)
        fk_ref[...] = fk.reshape(fk_ref.shape)
        fv_ref[...] = fv.reshape(fv_ref.shape)
        for s in range(n_steps):
            rows = slice(s * nb, (s + 1) * nb)
            fkt_ref[s] = fk[rows].T
            fvt_ref[s] = fv[rows].T
            lfft_ref[s] = lff[rows].T[:nh]
        return

    fvt = fv.T
    fqt = (fq * (LOG2E * dh ** -0.5)).T
    fkt_ref[...] = fk.T
    fvt_ref[...] = fvt
    lfft_ref[...] = lff.T[:nh]
    tri = (_iota((tm, tm), 0) >= _iota((tm, tm), 1)).astype(BF16)
    c_nat = _dot_exact_lhs(tri, _split3(lff))
    c_rel = c_nat * LOG2E
    c_rel_t = c_rel.T
    base = ccarry_ref[...]
    cbase_ref[...] = base * LOG2E
    ccarry_ref[...] = base + jnp.broadcast_to(c_nat[tm - 1:tm], ccarry_ref.shape)

    lane = _iota((tm, LANES), 1)
    srow = _iota((SUBLANES, tm), 0)
    npz = N_BIAS_PARTS
    pad_rows = jnp.zeros((LANES - dh - SUBLANES, tm), F32)
    c_parts = [pp.astype(F32) for pp in _split3(c_rel)]
    for hh in range(nh):
        ck = [jnp.broadcast_to(pp[:, hh:hh + 1], (tm, LANES)) for pp in c_parts]
        extra = jnp.where(lane < dh + npz, 1.0, 0.0)
        for j in range(npz):
            extra = jnp.where(lane == dh + npz + j, -ck[j], extra)
        ka_ref[:, hh * LANES:(hh + 1) * LANES] = jnp.where(lane < dh, _head_rows(fk, hh, dh), extra).astype(BF16)
        cq = [pp.astype(F32) for pp in _split3(c_rel_t[hh:hh + 1, :])]
        grp = jnp.where((srow >= npz) & (srow < 2 * npz), 1.0, 0.0)
        for j in range(npz):
            grp = jnp.where(srow == j, cq[j], grp)
        qat_ref[hh * LANES:(hh + 1) * LANES, :] = jnp.concatenate(
            [fqt[hh * dh:(hh + 1) * dh], grp, pad_rows], axis=0).astype(BF16)
        vat_ref[hh * LANES:(hh + 1) * LANES, :] = jnp.concatenate(
            [fvt[hh * dh:(hh + 1) * dh], jnp.where(srow == 0, 1.0, 0.0), pad_rows], axis=0).astype(BF16)


def _inproj(x, mod, ln_in, w_in_p, conv_w, lb_logits, fbias, prev, *, layer, n_seq, sample):
    t, d = x.shape
    tm = TOKEN_TILE
    n_tiles = t // tm
    seq = t // n_seq
    tiles_per_seq = seq // tm if not sample else 1
    n_steps = t // n_seq if sample else 0
    convw = conv_w.shape[1]
    hgw = lb_logits.shape[1]
    ncols = w_in_p.shape[1]
    foxw = (ncols - LANES - 3 * convw - 4 * hgw) // 3
    nh = foxw // HEAD_DIM
    first_layer = ln_in is not None
    depth = mod.shape[0]

    inputs, in_specs = [x], [pl.BlockSpec((tm, d), lambda i: (i, 0))]
    if sample:
        assert n_tiles == 1
        inputs.append(mod)
        in_specs.append(pl.BlockSpec((None, n_seq, 2 * d), lambda i: (layer, 0, 0)))
    else:
        rows = mod.shape[1]
        inputs.append(mod.reshape(depth, rows, 1, mod.shape[2]))
        in_specs.append(pl.BlockSpec((None, None, 1, 2 * d),
                                     lambda i: (layer, rows - SUBLANES + i // tiles_per_seq, 0, 0)))
    if first_layer:
        inputs += [ln_in[0].reshape(1, d), ln_in[1].reshape(1, d)]
        in_specs += [pl.BlockSpec((1, d), lambda i: (0, 0))] * 2
    inputs += [w_in_p, conv_w, lb_logits, fbias]
    in_specs += [
        pl.BlockSpec((d, ncols), lambda i: (0, 0), pipeline_mode=pl.Buffered(1)),
        pl.BlockSpec(conv_w.shape, lambda i: (0, 0)),
        pl.BlockSpec(lb_logits.shape, lambda i: (0, 0)),
        pl.BlockSpec((1, LANES), lambda i: (0, 0)),
    ]
    aliases = {}
    tok = lambda w, dt: (jax.ShapeDtypeStruct((t, w), dt), pl.BlockSpec((tm, w), lambda i: (i, 0)))
    whole = lambda shape, dt: (jax.ShapeDtypeStruct(shape, dt), pl.BlockSpec(shape, lambda i: (0,) * len(shape)))
    scratch = []
    if sample:
        inputs.append(prev)
        in_specs.append(pl.BlockSpec(prev.shape, lambda i: (0, 0)))
        outs = ([tok(convw, BF16), tok(convw, F32)] + [whole((n_steps, hgw, n_seq), F32)] * 4 + [tok(hgw, F32)]
                + [whole((n_steps, n_seq, foxw), F32)] * 3 + [whole((n_steps, foxw, n_seq), F32)] * 2
                + [whole((n_steps, nh, n_seq), F32)])
    else:
        per_seq_t = lambda rows, dt: (
            jax.ShapeDtypeStruct((n_seq, rows, seq), dt),
            pl.BlockSpec((None, rows, tm), lambda i: (i // tiles_per_seq, 0, i % tiles_per_seq)))
        all_layers = lambda rows: (
            jax.ShapeDtypeStruct((depth, n_seq, rows, seq), F32),
            pl.BlockSpec((None, None, rows, tm), lambda i: (layer, i // tiles_per_seq, 0, i % tiles_per_seq)))
        if prev is not None:
            aliases = {len(inputs) + j: FIRST_ALL_LAYER_OUTPUT + j for j in range(len(prev))}
            inputs += list(prev)
            in_specs += [pl.BlockSpec(memory_space=pl.ANY)] * len(prev)
        outs = ([tok(convw, BF16),
                 (jax.ShapeDtypeStruct((n_seq, SUBLANES, convw), F32),
                  pl.BlockSpec((None, SUBLANES, convw), lambda i: (i // tiles_per_seq, 0, 0)))]
                + [tok(hgw, F32)] * 5
                + [per_seq_t(nh * LANES, BF16), tok(nh * LANES, BF16), per_seq_t(nh * LANES, BF16),
                   all_layers(foxw), all_layers(foxw), all_layers(nh),
                   (jax.ShapeDtypeStruct((n_tiles, SUBLANES, LANES), F32),
                    pl.BlockSpec((None, SUBLANES, LANES), lambda i: (i, 0, 0)))])
        scratch = [pltpu.VMEM((SUBLANES, convw), F32), pltpu.VMEM((SUBLANES, LANES), F32)]
    kern = functools.partial(_inproj_kernel, layer=layer, sample=sample, first_layer=first_layer,
                             tiles_per_seq=tiles_per_seq, n_steps=n_steps, n_alias=len(aliases), d_model=d,
                             conv_w=convw,
                             hgrn_w=hgw, fox_w=foxw)
    return pl.pallas_call(
        kern,
        grid=(n_tiles,),
        in_specs=in_specs,
        out_specs=[o[1] for o in outs],
        out_shape=[o[0] for o in outs],
        scratch_shapes=scratch,
        input_output_aliases=aliases,
        compiler_params=_params(("arbitrary",)),
        name="inproj_sample" if sample else "inproj_prompt",
    )(*inputs)


def _hgrn_prompt_kernel(q_ref, k_ref, g_ref, v_ref, o_ref, st_out_ref, st_ref, *, chunk, n_heads):
    i = pl.program_id(1)
    tm, w = q_ref.shape
    dh = w // n_heads
    c = chunk

    @pl.when(i == 0)
    def _():
        st_ref[...] = jnp.zeros_like(st_ref)

    r_cc, c_cc = _iota((c, c), 0), _iota((c, c), 1)
    tri = (r_cc >= c_cc).astype(BF16)
    lane = _iota((1, w), 1)
    head_masks = [_div(lane, dh) == hh for hh in range(n_heads)]
    bd = _div(_iota((w, w), 0), dh) == _div(_iota((w, w), 1), dh)
    bd_bf = bd.astype(BF16)
    row = _iota((c, 1), 0)
    col_key = _mod(_iota((c, n_heads * c), 1), c)
    row_q = _iota((c, n_heads * c), 0)
    levels = []
    b = c // 2
    while b >= 1:
        levels.append(dict(
            half=b,
            upper=_mod(row, 2 * b) >= b,
            lower=_mod(row, 2 * b) < b,
            same=_div(row_q, 2 * b) == _div(col_key, 2 * b),
        ))
        b //= 2
    sub = _iota((1, SUBLANES, 1), 1)

    def boundary_rows(x, b):
        if 2 * b >= SUBLANES:
            return jnp.concatenate(
                [jnp.broadcast_to(x[s0 + b - 1:s0 + b], (2 * b, w)) for s0 in range(0, c, 2 * b)], axis=0)
        x3 = x.reshape(c // SUBLANES, SUBLANES, w)
        out = None
        for s0 in range(SUBLANES - 2 * b, -1, -2 * b):
            rep = jnp.broadcast_to(x3[:, s0 + b - 1:s0 + b, :], x3.shape)
            out = rep if out is None else jnp.where(sub < s0 + 2 * b, rep, out)
        return out.reshape(c, w)

    chunks = range(tm // c)
    rows = [slice(ci * c, (ci + 1) * c) for ci in chunks]
    q = [q_ref[r, :] for r in rows]
    k = [k_ref[r, :] for r in rows]
    v = [v_ref[r, :] for r in rows]
    vb = [x.astype(BF16) for x in v]
    gcum = [_dot_exact_lhs(tri, _split3(g_ref[r, :])) * LOG2E for r in rows]
    a = [jnp.zeros((c, n_heads * c), F32) for _ in chunks]
    for lv in levels:
        for ci in chunks:
            d = gcum[ci] - boundary_rows(gcum[ci], lv["half"])
            e = jnp.exp2(jnp.where(lv["upper"], d, -d))
            qt = jnp.where(lv["upper"], q[ci] * e, 0.0).astype(BF16)
            kt = jnp.where(lv["lower"], k[ci] * e, 0.0).astype(BF16)
            kbd = jnp.concatenate([jnp.where(hm, kt, jnp.zeros_like(kt)) for hm in head_masks], axis=0)
            xl = lax.dot_general(qt, kbd, NT_DIMS, preferred_element_type=F32)
            a[ci] = jnp.where(lv["same"], xl, a[ci])
    o = []
    for ci in chunks:
        diag = _dot_exact_rhs(_split2(q[ci] * k[ci]), bd_bf)
        vbd = jnp.concatenate([jnp.where(hm, vb[ci], jnp.zeros_like(vb[ci])) for hm in head_masks], axis=0)
        o.append(diag * v[ci] + jnp.dot(a[ci].astype(BF16), vbd, preferred_element_type=F32))
    st = st_ref[...]
    for ci in chunks:
        qg = (q[ci] * jnp.exp2(gcum[ci])).astype(BF16)
        o_ref[rows[ci], :] = o[ci] + lax.dot_general(qg, st.astype(BF16), NT_DIMS, preferred_element_type=F32)
        g_last = gcum[ci][c - 1:c]
        kd = (k[ci] * jnp.exp2(g_last - gcum[ci])).astype(BF16)
        upd = lax.dot_general(vb[ci], kd, TN_DIMS, preferred_element_type=F32)
        st = st * jnp.exp2(g_last) + jnp.where(bd, upd, 0.0)
    st_ref[...] = st

    @pl.when(i == pl.num_programs(1) - 1)
    def _():
        st_out_ref[...] = st_ref[...].T


def _hgrn_prompt(qh, kh, lfh, vh, *, n_seq):
    t, w = qh.shape
    tm = HGRN_TILE
    tps = (t // n_seq) // tm
    spec = pl.BlockSpec((tm, w), lambda b, i: (b * tps + i, 0))
    kern = functools.partial(_hgrn_prompt_kernel, chunk=HGRN_CHUNK, n_heads=HGRN_HEADS)
    return pl.pallas_call(
        kern,
        grid=(n_seq, tps),
        in_specs=[spec] * 4,
        out_specs=[spec, pl.BlockSpec((None, w, w), lambda b, i: (b, 0, 0))],
        out_shape=[jax.ShapeDtypeStruct((t, w), F32), jax.ShapeDtypeStruct((n_seq, w, w), F32)],
        scratch_shapes=[pltpu.VMEM((w, w), F32)],
        compiler_params=_params(("arbitrary", "arbitrary")),
        name="hgrn_prompt",
    )(qh, kh, lfh, vh)


def _hgrn_sample_kernel(q_ref, k_ref, g_ref, v_ref, s0_ref, o_ref, s_out_ref):
    n_steps, dh, nb = q_ref.shape

    def body(d, o_acc):
        s = s0_ref[d]
        out = []
        for t in range(n_steps):
            row = pl.ds(d, 1)
            s = jnp.exp(g_ref[t, row, :]) * s + k_ref[t, row, :] * v_ref[t]
            out.append(o_acc[t] + q_ref[t, row, :] * s)
        s_out_ref[d] = s
        return tuple(out)

    o = lax.fori_loop(0, dh, body, tuple(jnp.zeros((v_ref.shape[1], nb), F32) for _ in range(n_steps)))
    for t in range(n_steps):
        o_ref[t] = o[t]


def _hgrn_sample(qh_t, kh_t, lfh_t, vh_t, state_t, *, layer):
    n_steps, w, nb = qh_t.shape
    _, n_heads, dk, dv, _ = state_t.shape
    gate = pl.BlockSpec((n_steps, dk, nb), lambda h: (0, h, 0))
    val = pl.BlockSpec((n_steps, dv, nb), lambda h: (0, h, 0))
    return pl.pallas_call(
        _hgrn_sample_kernel,
        grid=(n_heads,),
        in_specs=[gate, gate, gate, val,
                  pl.BlockSpec((None, None, dk, dv, nb), lambda h: (layer, h, 0, 0, 0))],
        out_specs=[val, pl.BlockSpec((None, dk, dv, nb), lambda h: (h, 0, 0, 0))],
        out_shape=[jax.ShapeDtypeStruct((n_steps, n_heads * dv, nb), F32),
                   jax.ShapeDtypeStruct((n_heads, dk, dv, nb), F32)],
        compiler_params=_params(("arbitrary",)),
        name="hgrn_sample",
    )(qh_t, kh_t, lfh_t, vh_t, state_t)


def _fox_prompt_kernel(qi_ref, ka_idx_ref, kb_idx_ref, kind_ref, last_ref, ka_a, ka_b, qat_ref, vat_a, vat_b,
                       cbq_ref, cbk_a, cbk_b, o_ref, m_scr, acc_scr, s_scr, mask_scr):
    step = pl.program_id(1)
    kind = kind_ref[step]
    tk, tq = ka_a.shape[0], qat_ref.shape[1]
    nh = m_scr.shape[0]
    dh = HEAD_DIM

    @pl.when((pl.program_id(0) == 0) & (step == 0))
    def _():
        mask_scr[...] = jnp.where(_iota((tk, tq), 0) <= _iota((tk, tq), 1), 0.0, NEG_BIG)

    @pl.when(ka_idx_ref[step] == 0)
    def _():
        m_scr[...] = jnp.full_like(m_scr, NEG_BIG)
        acc_scr[...] = jnp.zeros_like(acc_scr)

    kas, vats = (ka_a, ka_b), (vat_a, vat_b)
    deltas = (cbq_ref[0:1, :] - cbk_a[0:1, :], cbq_ref[0:1, :] - cbk_b[0:1, :])
    nbuf = s_scr.shape[0]

    def tiles(masked):
        items = [(t, hh) for t in range(len(masked)) for hh in range(nh)]

        def scores(n):
            t, hh = items[n]
            ls = slice(hh * LANES, (hh + 1) * LANES)
            s = jnp.dot(kas[t][:, ls], qat_ref[ls, :], preferred_element_type=F32)
            if masked[t]:
                s = s + mask_scr[...]
            s_scr[n % nbuf] = s

        for n in range(min(nbuf - 1, len(items))):
            scores(n)
        for n, (t, hh) in enumerate(items):
            if n + nbuf - 1 < len(items):
                scores(n + nbuf - 1)
            ls = slice(hh * LANES, (hh + 1) * LANES)
            s = s_scr[n % nbuf]
            d = deltas[t][:, hh:hh + 1]
            m_prev = m_scr[hh]
            m_next = jnp.maximum(m_prev, jnp.max(s, axis=0, keepdims=True) + d)
            p = jnp.exp2(s - (m_next[0:1] - d))
            alpha = jnp.exp2(m_prev - m_next)
            pv = jnp.dot(vats[t][ls, :], p.astype(BF16), preferred_element_type=F32)
            acc_scr[hh] = acc_scr[hh] * alpha[0:1] + pv
            m_scr[hh] = m_next

    pl.when(kind == 0)(lambda: tiles((False, False)))
    pl.when(kind == 1)(lambda: tiles((False, True)))
    pl.when(kind == 2)(lambda: tiles((True,)))

    @pl.when(last_ref[step] == 1)
    def _():
        per = LANES // dh
        for slab in range(nh // per):
            vals = []
            for j in range(per):
                acc = acc_scr[slab * per + j]
                vals.append(acc[:dh] / acc[dh:dh + 1])
            o_ref[:, slab * LANES:(slab + 1) * LANES] = jnp.concatenate(vals, axis=0).T.astype(BF16)


def _fox_prompt(ka, qat, vat, cbase, *, n_seq, n_heads):
    t, wa = ka.shape
    tq = tk = TOKEN_TILE
    nq = (t // n_seq) // tq
    steps = []
    for i in range(nq):
        for a in range(0, i + 1, 2):
            two = a + 1 <= i
            steps.append((i, a, a + 1 if two else a, (1 if a + 1 == i else 0) if two else 2,
                          int(a + 1 >= i)))
    tabs = [jnp.asarray([st[c] for st in steps], jnp.int32) for c in range(5)]
    k_rows = lambda tab: pl.BlockSpec((tk, wa), lambda b, p, qi, ka_i, kb_i, kind, last: (b * nq + (ka_i, kb_i)[tab][p], 0))
    k_cols = lambda tab: pl.BlockSpec((None, wa, tk), lambda b, p, qi, ka_i, kb_i, kind, last: (b, 0, (ka_i, kb_i)[tab][p]))
    k_base = lambda tab: pl.BlockSpec((None, SUBLANES, LANES),
                                      lambda b, p, qi, ka_i, kb_i, kind, last: (b * nq + (ka_i, kb_i)[tab][p], 0, 0))
    grid_spec = pltpu.PrefetchScalarGridSpec(
        num_scalar_prefetch=5,
        grid=(n_seq, len(steps)),
        in_specs=[
            k_rows(0), k_rows(1),
            pl.BlockSpec((None, wa, tq), lambda b, p, qi, ka_i, kb_i, kind, last: (b, 0, qi[p])),
            k_cols(0), k_cols(1),
            pl.BlockSpec((None, SUBLANES, LANES), lambda b, p, qi, ka_i, kb_i, kind, last: (b * nq + qi[p], 0, 0)),
            k_base(0), k_base(1),
        ],
        out_specs=pl.BlockSpec((tq, n_heads * HEAD_DIM), lambda b, p, qi, ka_i, kb_i, kind, last: (b * nq + qi[p], 0)),
        scratch_shapes=[pltpu.VMEM((n_heads, SUBLANES, tq), F32), pltpu.VMEM((n_heads, LANES, tq), F32),
                        pltpu.VMEM((FOX_SCORE_BUFFERS, tk, tq), F32), pltpu.VMEM((tk, tq), F32)],
    )
    return pl.pallas_call(
        _fox_prompt_kernel,
        grid_spec=grid_spec,
        out_shape=jax.ShapeDtypeStruct((t, n_heads * HEAD_DIM), BF16),
        compiler_params=_params(("arbitrary", "arbitrary")),
        name="fox_prompt",
    )(*tabs, ka, ka, qat, vat, vat, cbase, cbase, cbase)


def _fox_sample_kernel(pt_ref, *refs, n_pages, n_steps, n_heads):
    del pt_ref
    kp = refs[:n_pages]
    vp = refs[n_pages:2 * n_pages]
    lp = refs[2 * n_pages:3 * n_pages]
    q_ref, kn_ref, vn_ref, lfn_ref, o_ref, knew_scr, vnew_scr = refs[3 * n_pages:]
    _, dh, page = kp[0].shape
    w = n_heads * dh
    rows = n_steps * n_heads
    b = pl.program_id(0)
    r = pl.ds(_mod(b, SUBLANES), 1)

    @pl.when(b == 0)
    def _():
        knew_scr[...] = jnp.zeros_like(knew_scr)
        vnew_scr[...] = jnp.zeros_like(vnew_scr)

    hmask = _div(_iota((n_heads, w), 1), dh) == _iota((n_heads, w), 0)
    qbd = []
    for s in range(n_steps):
        knew_scr[s:s + 1, :] = kn_ref[s, r, :]
        vnew_scr[s:s + 1, :] = vn_ref[s, r, :]
        qbd.append(jnp.where(hmask, jnp.broadcast_to(q_ref[s, r, :], (n_heads, w)), 0.0))
    qbd = jnp.concatenate(qbd, axis=0).astype(BF16)

    lf_all = jnp.concatenate([lp[j][...] for j in range(n_pages)], axis=0)
    parts = _split3(lf_all)
    after = (_iota((page, page), 0) > _iota((page, page), 1)).astype(BF16)
    ones = jnp.ones((page, page), BF16)
    excl = sum(jnp.dot(pp, after, preferred_element_type=F32) for pp in parts)
    total = sum(jnp.dot(pp, ones, preferred_element_type=F32) for pp in parts)
    scores = [None] * n_pages
    later = jnp.zeros((n_heads, page), F32)
    for j in range(n_pages - 1, -1, -1):
        ps = slice(j * n_heads, (j + 1) * n_heads)
        bias = excl[ps] + later
        later = later + total[ps]
        kt = kp[j][...].reshape(w, page).astype(BF16)
        s = jnp.dot(qbd, kt, preferred_element_type=F32)
        scores[j] = s + jnp.concatenate([bias] * n_steps, axis=0)
    lane_h = _iota((n_heads, page), 1)
    seq_lane = _iota(lfn_ref.shape[1:], 1) == b
    bias_new = jnp.zeros((n_heads, page), F32)
    cum = jnp.zeros((n_heads, 1), F32)
    for s in range(n_steps):
        cum = cum + jnp.sum(jnp.where(seq_lane, lfn_ref[s], 0.0), axis=1, keepdims=True)
        bias_new = jnp.where(lane_h == s, -cum, bias_new)
    s_new = lax.dot_general(qbd, knew_scr[...].astype(BF16), NT_DIMS, preferred_element_type=F32)
    s_new = s_new + jnp.concatenate([bias_new] * n_steps, axis=0)
    key_pos = _iota((rows, page), 1)
    q_step = _div(_iota((rows, page), 0), n_heads)
    scores.append(jnp.where(key_pos <= q_step, s_new, NEG_BIG))

    m = scores[0].max(axis=1, keepdims=True)
    for s in scores[1:]:
        m = jnp.maximum(m, s.max(axis=1, keepdims=True))
    acc = jnp.zeros((rows, w), F32)
    den = jnp.zeros((rows, 1), F32)
    for j, s in enumerate(scores):
        p = jnp.exp(s - m)
        den = den + p.sum(axis=1, keepdims=True)
        pb = p.astype(BF16)
        if j < n_pages:
            vt = vp[j][...].reshape(w, page).astype(BF16)
            acc = acc + lax.dot_general(pb, vt, NT_DIMS, preferred_element_type=F32)
        else:
            acc = acc + jnp.dot(pb, vnew_scr[...].astype(BF16), preferred_element_type=F32)
    out = acc / den
    for s in range(n_steps):
        blk = jnp.where(hmask, out[s * n_heads:(s + 1) * n_heads], 0.0)
        o_ref[s, r, :] = blk.sum(axis=0, keepdims=True)


def _fox_sample(page_table, ck_t, cv_t, clf_t, fq, fk, fv, lff_t, *, layer):
    nb, n_pages = page_table.shape
    _, n_phys, n_heads, dh, page = ck_t.shape
    n_steps = fq.shape[0]
    w = n_heads * dh
    pt_flat = page_table.reshape(-1)

    def page_spec(shape, j):
        nd = len(shape)
        return pl.BlockSpec((None, None) + shape, lambda b, pt: (layer, pt[b * n_pages + j]) + (0,) * nd)

    in_specs = ([page_spec((n_heads, dh, page), j) for j in range(n_pages)] * 2
                + [page_spec((n_heads, page), j) for j in range(n_pages)])
    step_spec = pl.BlockSpec((n_steps, SUBLANES, w), lambda b, pt: (0, b // SUBLANES, 0))
    in_specs += [step_spec] * 3 + [pl.BlockSpec(lff_t.shape, lambda b, pt: (0, 0, 0))]
    kern = functools.partial(_fox_sample_kernel, n_pages=n_pages, n_steps=n_steps, n_heads=n_heads)
    grid_spec = pltpu.PrefetchScalarGridSpec(
        num_scalar_prefetch=1,
        grid=(nb,),
        in_specs=in_specs,
        out_specs=step_spec,
        scratch_shapes=[pltpu.VMEM((page, w), F32), pltpu.VMEM((page, w), F32)],
    )
    return pl.pallas_call(
        kern,
        grid_spec=grid_spec,
        out_shape=jax.ShapeDtypeStruct((n_steps, nb, w), F32),
        compiler_params=_params(("arbitrary",)),
        name="fox_sample",
    )(pt_flat, *([ck_t] * n_pages), *([cv_t] * n_pages), *([clf_t] * n_pages), fq, fk, fv, lff_t)


def _outffn_kernel(*refs, sample, first_layer, d_model, n_heads, alpha):
    it = iter(refs)
    x_ref, yc_ref, oh_ref, hg_ref, yf_ref, ma_ref, mb_ref = (next(it) for _ in range(7))
    if first_layer:
        lng_ref, lnb_ref = next(it), next(it)
    ng_ref, wo_ref, wg_ref, wu_ref, wd_ref, lg_ref, lb_ref, out_ref, acc_ref = (next(it) for _ in range(9))

    tm = x_ref.shape[0]
    x = x_ref[...]
    if first_layer:
        x = _layer_norm(x, lng_ref[...], lnb_ref[...])
    ma, mb = ma_ref[...], mb_ref[...]
    if sample:
        reps = tm // ma.shape[0]
        ma = jnp.concatenate([ma] * reps, axis=0)
        mb = jnp.concatenate([mb] * reps, axis=0)
        o = jnp.concatenate([oh_ref[s].T for s in range(oh_ref.shape[0])], axis=0)
        yf = yf_ref[...].reshape(tm, yf_ref.shape[2]).astype(BF16)
    else:
        o = oh_ref[...]
        yf = yf_ref[...]
    g1, sh2 = ma[:, :d_model], ma[:, d_model:]
    sc2, g2 = mb[:, :d_model], mb[:, d_model:]

    w = o.shape[1]
    dh = w // n_heads
    bd = (_div(_iota((w, w), 0), dh) == _div(_iota((w, w), 1), dh)).astype(BF16)
    ms = _dot_exact_rhs(_split2(o * o), bd) * (1.0 / dh)
    hg = hg_ref[...]
    yh = (o * lax.rsqrt(ms + RMS_EPS) * ng_ref[...] * (hg * jax.nn.sigmoid(hg))).astype(BF16)

    mixed = jnp.concatenate([yc_ref[...], yh, yf], axis=1)
    mix = jnp.dot(mixed, wo_ref[...], preferred_element_type=F32)
    x1 = _layer_norm(alpha * x + (1.0 + g1) * mix, lg_ref[0:1, :], lb_ref[0:1, :])

    h2 = (x1 * (1.0 + sc2) + sh2).astype(BF16)
    acc_ref[...] = jnp.zeros_like(acc_ref)

    def ffn_chunk(ci, carry):
        gate = jnp.dot(h2, wg_ref[ci], preferred_element_type=F32)
        up = jnp.dot(h2, wu_ref[ci], preferred_element_type=F32)
        act = (gate * jax.nn.sigmoid(gate) * up).astype(BF16)
        acc_ref[...] += jnp.dot(act, wd_ref[ci], preferred_element_type=F32)
        return carry

    lax.fori_loop(0, wg_ref.shape[0], ffn_chunk, 0)
    out_ref[...] = _layer_norm(alpha * x1 + (1.0 + g2) * acc_ref[...], lg_ref[1:2, :], lb_ref[1:2, :])


def _outffn(x, yconv, oh, hg, yfox, mod, ln_in, norm_g, w_out_b, wg, wu, wd, ln_g, ln_b,
            *, layer, n_seq, sample, alpha):
    t, d = x.shape
    tm = TOKEN_TILE if sample else OUTFFN_TILE
    first_layer = ln_in is not None
    depth, rows, _ = mod.shape
    tiles_per_seq = (t // n_seq) // tm if not sample else 1
    tok = lambda a: pl.BlockSpec((tm, a.shape[1]), lambda i: (i, 0))
    const = lambda a: pl.BlockSpec(a.shape, lambda i: (0,) * a.ndim, pipeline_mode=pl.Buffered(1))
    small = lambda a: pl.BlockSpec(a.shape, lambda i: (0,) * a.ndim)

    inputs = [x, yconv, oh, hg, yfox]
    if sample:
        in_specs = [tok(x), tok(yconv), small(oh), tok(hg), small(yfox)]
        inputs += [mod, mod]
        in_specs += [pl.BlockSpec((None, n_seq, 2 * d), lambda i: (layer, 0, 1)),
                     pl.BlockSpec((None, n_seq, 2 * d), lambda i: (layer, 0, 2))]
    else:
        in_specs = [tok(a) for a in inputs]
        m4 = mod.reshape(depth, rows, 1, mod.shape[2])
        inputs += [m4, m4]
        row_of = lambda i: rows - SUBLANES + i // tiles_per_seq
        in_specs += [pl.BlockSpec((None, None, 1, 2 * d), lambda i: (layer, row_of(i), 0, 1)),
                     pl.BlockSpec((None, None, 1, 2 * d), lambda i: (layer, row_of(i), 0, 2))]
    if first_layer:
        inputs += [ln_in[0].reshape(1, d), ln_in[1].reshape(1, d)]
        in_specs += [pl.BlockSpec((1, d), lambda i: (0, 0))] * 2
    ng = norm_g.reshape(1, -1)
    inputs += [ng, w_out_b, wg, wu, wd, ln_g, ln_b]
    in_specs += [small(ng), const(w_out_b), const(wg), const(wu), const(wd), small(ln_g), small(ln_b)]
    kern = functools.partial(_outffn_kernel, sample=sample, first_layer=first_layer, d_model=d,
                             n_heads=HGRN_HEADS, alpha=alpha)
    return pl.pallas_call(
        kern,
        grid=(t // tm,),
        in_specs=in_specs,
        out_specs=pl.BlockSpec((tm, d), lambda i: (i, 0)),
        out_shape=jax.ShapeDtypeStruct((t, d), F32),
        scratch_shapes=[pltpu.VMEM((tm, d), F32)],
        compiler_params=_params(("arbitrary",)),
        name="outffn_sample" if sample else "outffn_prompt",
    )(*inputs)


def kernel(x_prompt, x_sample, c_prompt, c_sample, cache_k, cache_v, cache_logf, state_hgrn, state_conv,
           page_table, ln_in_g, ln_in_b, w_mod, b_mod, w_in, conv_w, hgrn_lb_logits, hgrn_norm_g,
           fox_f_bias, w_out, w_ffn_in, w_ffn_out, ln_g, ln_b):
    nbp, seq, d = x_prompt.shape
    nbs, n_steps, _ = x_sample.shape
    depth = w_mod.shape[0]
    alpha = (2 * depth) ** 0.25
    nh, dh = cache_k.shape[3:]
    dff = w_ffn_out.shape[1]
    nchunk = dff // FFN_CHUNK
    hgw = hgrn_lb_logits.shape[1]
    hdk = hgw // HGRN_HEADS

    c_all = jnp.concatenate([c_sample, c_prompt, jnp.zeros((SUBLANES - nbp, d), F32)], axis=0)
    mod = _modulation(c_all, w_mod, b_mod)

    ln_in = (ln_in_g, ln_in_b)
    xp = x_prompt.reshape(nbp * seq, d)
    xs = x_sample.transpose(1, 0, 2).reshape(n_steps * nbs, d)

    ck_t = cache_k.transpose(0, 1, 3, 4, 2)
    cv_t = cache_v.transpose(0, 1, 3, 4, 2)
    clf_t = cache_logf.transpose(0, 1, 3, 2)
    state_t = state_hgrn.transpose(0, 2, 3, 4, 1)

    outs_p = dict(s=[], conv=[])
    all_kvl = None
    outs_s = dict(k=[], v=[], lf=[], s=[], conv=[])
    for l in range(depth):
        w_in_p = jnp.pad(w_in[l], ((0, 0), (0, LANES - nh))).astype(BF16)
        fbias = jnp.pad(fox_f_bias[l], (0, LANES - nh)).reshape(1, LANES)
        w_out_b = w_out[l].astype(BF16)
        wg = w_ffn_in[l][:, :dff].reshape(d, nchunk, FFN_CHUNK).transpose(1, 0, 2).astype(BF16)
        wu = w_ffn_in[l][:, dff:].reshape(d, nchunk, FFN_CHUNK).transpose(1, 0, 2).astype(BF16)
        wd = w_ffn_out[l].reshape(nchunk, FFN_CHUNK, d).astype(BF16)
        lin = ln_in if l == 0 else None

        (yconv, ulast, qh, kh, lfh, vh, hg, qat, ka, vat, fkt, fvt, lfft, cbase) = _inproj(
            xp, mod, lin, w_in_p, conv_w[l], hgrn_lb_logits, fbias, all_kvl, layer=l, n_seq=nbp, sample=False)
        all_kvl = (fkt, fvt, lfft)
        oh, st_t = _hgrn_prompt(qh, kh, lfh, vh, n_seq=nbp)
        yfox = _fox_prompt(ka, qat, vat, cbase, n_seq=nbp, n_heads=nh)
        xp = _outffn(xp, yconv, oh, hg, yfox, mod, lin, hgrn_norm_g[l], w_out_b, wg, wu, wd,
                     ln_g[l], ln_b[l], layer=l, n_seq=nbp, sample=False, alpha=alpha)
        st4 = st_t.reshape(nbp, HGRN_HEADS, hdk, HGRN_HEADS, hdk)
        outs_p["s"].append(jnp.stack([st4[:, hh, :, hh, :] for hh in range(HGRN_HEADS)], axis=1))
        outs_p["conv"].append(ulast[:, SUBLANES - (CONV_K - 1):, :])

        prev = state_conv[l].transpose(1, 0, 2).reshape((CONV_K - 1) * nbs, -1)
        (yconv, u, qh, kh, lfh, vh, hg, fq, fk, fv, fkt, fvt, lfft) = _inproj(
            xs, mod, lin, w_in_p, conv_w[l], hgrn_lb_logits, fbias, prev, layer=l, n_seq=nbs, sample=True)
        oh, s_fin = _hgrn_sample(qh, kh, lfh, vh, state_t, layer=l)
        yfox = _fox_sample(page_table, ck_t, cv_t, clf_t, fq, fk, fv, lfft, layer=l)
        xs = _outffn(xs, yconv, oh, hg, yfox, mod, lin, hgrn_norm_g[l], w_out_b, wg, wu, wd,
                     ln_g[l], ln_b[l], layer=l, n_seq=nbs, sample=True, alpha=alpha)
        outs_s["s"].append(s_fin)
        outs_s["k"].append(fkt)
        outs_s["v"].append(fvt)
        outs_s["lf"].append(lfft)
        outs_s["conv"].append(u.reshape(n_steps, nbs, -1)[n_steps - (CONV_K - 1):].transpose(1, 0, 2))

    st = lambda xs_: jnp.stack(xs_)
    y_prompt = xp.reshape(nbp, seq, d)
    y_sample = xs.reshape(n_steps, nbs, d).transpose(1, 0, 2)
    kv_p = lambda a: a.reshape(depth, nbp, nh, dh, seq).transpose(0, 1, 4, 2, 3)
    kv_s = lambda a: st(a).reshape(depth, n_steps, nh, dh, nbs).transpose(0, 4, 1, 2, 3)
    return (y_prompt, y_sample,
            kv_p(all_kvl[0]), kv_p(all_kvl[1]), all_kvl[2].transpose(0, 1, 3, 2),
            st(outs_p["s"]), st(outs_p["conv"]),
            kv_s(outs_s["k"]), kv_s(outs_s["v"]), st(outs_s["lf"]).transpose(0, 3, 1, 2),
            st(outs_s["s"]).transpose(0, 4, 1, 2, 3), st(outs_s["conv"]))
```
